```python
import math
import jax, jax.numpy as jnp
from jax import lax
import numpy as np

D_MODEL = 2048
BATCH = 4
SEQ = 4096
DEPTH = 2

EXPAND = 2
D_INNER = EXPAND * D_MODEL
HEAD_DIM = 128
N_HEADS_SB = (D_INNER // 2) // HEAD_DIM
N_HEADS_DIL = (D_INNER // 2) // HEAD_DIM
D_SB = N_HEADS_SB * HEAD_DIM
D_DIL = N_HEADS_DIL * HEAD_DIM
DILATED_GROUPS = ((128, 1), (512, 4), (2048, 16))
SB_BLOCK = 128
N_HEADS_HGRN = D_INNER // HEAD_DIM
HGRN_CHUNK = 32
N_EVEN = (DEPTH + 1) // 2
N_ODD = DEPTH // 2
EVEN_IN_COLS = 3 * D_SB + 3 * D_DIL + D_INNER
ODD_IN_COLS = 4 * D_INNER
RMS_EPS = 1e-6

kernel_name = "hybrid_stickbreak_dilated_hgrn2"


def rms_norm(x, gain):
    xf = x.astype(jnp.float32)
    y = xf * lax.rsqrt(jnp.mean(xf * xf, axis=-1, keepdims=True) + RMS_EPS)
    return (y * gain.astype(jnp.float32)).astype(x.dtype)


def split_cols(t, sizes):
    idx = [int(c) for c in np.cumsum(sizes)[:-1]]
    return jnp.split(t, idx, axis=-1)


def to_heads(t):
    b, s, _ = t.shape
    return t.reshape(b, s, -1, HEAD_DIM)


def stick_breaking_attention(q, k, v):
    b, s_len, h, dh = q.shape
    scale = 1.0 / math.sqrt(dh)
    qf = q.astype(jnp.float32).transpose(0, 2, 1, 3)
    kf = k.astype(jnp.float32).transpose(0, 2, 1, 3)
    vf = v.astype(jnp.float32).transpose(0, 2, 1, 3)
    outs = []
    for blk in range(s_len // SB_BLOCK):
        q0 = blk * SB_BLOCK
        kend = q0 + SB_BLOCK
        z = jnp.einsum('bhqd,bhkd->bhqk', qf[:, :, q0:kend], kf[:, :, :kend]) * scale
        t_pos = q0 + jnp.arange(SB_BLOCK)[:, None]
        s_pos = jnp.arange(kend)[None, :]
        mask = s_pos < t_pos
        log_1m = jnp.where(mask, jax.nn.log_sigmoid(-z), 0.0)
        tail = lax.cumsum(log_1m, axis=3, reverse=True) - log_1m
        a = jnp.where(mask, jnp.exp(jax.nn.log_sigmoid(z) + tail), 0.0)
        outs.append(jnp.einsum('bhqk,bhkd->bhqd', a, vf[:, :, :kend]))
    return jnp.concatenate(outs, axis=2).transpose(0, 2, 1, 3)


def dilated_branch(q, k, v, window, dilation):
    b, s_len, h, dh = q.shape
    scale = 1.0 / math.sqrt(dh)
    blk = window // dilation
    L = s_len // dilation
    nblk = -(-L // blk)
    Lp = nblk * blk

    def to_blocks(t):
        t = t.reshape(b, L, dilation, h, dh).transpose(0, 2, 1, 3, 4)
        t = jnp.pad(t, ((0, 0), (0, 0), (0, Lp - L), (0, 0), (0, 0)))
        return t.reshape(b, dilation, nblk, blk, h, dh)

    def with_prev(t):
        prev = jnp.pad(t, ((0, 0), (0, 0), (1, 0), (0, 0), (0, 0), (0, 0)))[:, :, :-1]
        return jnp.concatenate([prev, t], axis=3)

    qb = to_blocks(q)
    kc = with_prev(to_blocks(k))
    vc = with_prev(to_blocks(v))
    sc = jnp.einsum('bpnqhd,bpnkhd->bpnhqk', qb, kc) * scale
    n_i = jnp.arange(nblk)[:, None, None]
    q_i = jnp.arange(blk)[None, :, None]
    k_j = jnp.arange(2 * blk)[None, None, :]
    dist = blk + q_i - k_j
    valid = (dist >= 0) & (dist <= blk) & ((n_i > 0) | (k_j >= blk))
    sc = jnp.where(valid[:, None], sc, -jnp.inf)
    lse = jax.nn.logsumexp(sc, axis=-1)
    p = jnp.exp(sc - lse[..., None])
    o = jnp.einsum('bpnhqk,bpnkhd->bpnqhd', p, vc)
    o = o.reshape(b, dilation, Lp, h, dh)[:, :, :L].transpose(0, 2, 1, 3, 4).reshape(b, s_len, h, dh)
    lse = lse.transpose(0, 1, 2, 4, 3).reshape(b, dilation, Lp, h)[:, :, :L]
    lse = lse.transpose(0, 2, 1, 3).reshape(b, s_len, h)
    return o, lse


def dilated_attention(q, k, v, q_gain, k_gain):
    qf = rms_norm(q.astype(jnp.float32), q_gain)
    kf = rms_norm(k.astype(jnp.float32), k_gain)
    vf = v.astype(jnp.float32)
    outs, lses = [], []
    for window, dilation in DILATED_GROUPS:
        o, lse = dilated_branch(qf, kf, vf, window, dilation)
        outs.append(o)
        lses.append(lse)
    wts = jax.nn.softmax(jnp.stack(lses, axis=0), axis=0)
    return jnp.sum(wts[..., None] * jnp.stack(outs, axis=0), axis=0)


def hgrn2(q, f_pre, inp, lb):
    b, s_len, h, d = q.shape
    lb = lb.reshape(h, d)
    fp = f_pre.astype(jnp.float32)
    log_f = jnp.logaddexp(jnp.log(lb), jnp.log1p(-lb) + jax.nn.log_sigmoid(fp))
    key = (1.0 - lb) * jax.nn.sigmoid(-fp)
    n_c = s_len // HGRN_CHUNK

    def chunks(t):
        return t.astype(jnp.float32).reshape(b, n_c, HGRN_CHUNK, h, d).transpose(1, 0, 3, 2, 4)

    causal = jnp.arange(HGRN_CHUNK)[:, None] >= jnp.arange(HGRN_CHUNK)[None, :]

    def step(state, xs):
        qc, kc, vc, gc = xs
        cum = jnp.cumsum(gc, axis=2)
        inter = jnp.einsum('bhtc,bhcv->bhtv', qc * jnp.exp(cum), state)
        diff = cum[:, :, :, None, :] - cum[:, :, None, :, :]
        decay = jnp.exp(jnp.where(causal[:, :, None], diff, -jnp.inf))
        attn = jnp.einsum('bhtc,bhsc,bhtsc->bhts', qc, kc, decay)
        intra = jnp.einsum('bhts,bhsv->bhtv', attn, vc)
        last = cum[:, :, -1:, :]
        new_state = jnp.exp(last[:, :, 0, :])[..., None] * state + jnp.einsum(
            'bhsc,bhsv->bhcv', kc * jnp.exp(last - cum), vc)
        return new_state, inter + intra

    init = jnp.zeros((b, h, d, d), jnp.float32)
    _, ys = lax.scan(step, init, (chunks(q), chunks(key), chunks(inp), chunks(log_f)))
    return ys.transpose(1, 0, 3, 2, 4).reshape(b, s_len, h, d)


def even_layer(x, norm_g, w_in, q_gain, k_gain, w_out):
    b, s_len, _ = x.shape
    hdn = rms_norm(x, norm_g)
    proj = jnp.einsum('bsd,de->bse', hdn, w_in)
    qa, ka, va, qb, kb, vb, gate = split_cols(
        proj, [D_SB, D_SB, D_SB, D_DIL, D_DIL, D_DIL, D_INNER])
    oa = stick_breaking_attention(to_heads(qa), to_heads(ka), to_heads(va))
    ob = dilated_attention(to_heads(qb), to_heads(kb), to_heads(vb), q_gain, k_gain)
    mix = jnp.concatenate([oa.reshape(b, s_len, D_SB), ob.reshape(b, s_len, D_DIL)], axis=-1)
    mix = mix * jax.nn.silu(gate.astype(jnp.float32))
    return x + jnp.einsum('bse,ed->bsd', mix.astype(x.dtype), w_out)


def odd_layer(x, norm_g, w_in, lb, o_gain, w_out):
    b, s_len, _ = x.shape
    hdn = rms_norm(x, norm_g)
    proj = jnp.einsum('bsd,de->bse', hdn, w_in)
    q, f_pre, inp, gate = split_cols(proj, [D_INNER, D_INNER, D_INNER, D_INNER])
    o = hgrn2(to_heads(q), to_heads(f_pre), to_heads(inp), lb)
    o = rms_norm(o, o_gain).reshape(b, s_len, D_INNER)
    mix = o * jax.nn.silu(gate.astype(jnp.float32))
    return x + jnp.einsum('bse,ed->bsd', mix.astype(x.dtype), w_out)


def setup_inputs(seed: int = 0) -> dict:
    key = jax.random.key(seed)
    ks = jax.random.split(key, 12)
    f32 = jnp.float32
    x = jax.random.normal(ks[0], (BATCH, SEQ, D_MODEL), f32)
    norm_even = 1.0 + 0.02 * jax.random.normal(ks[1], (N_EVEN, D_MODEL), f32)
    w_in_even = jax.random.normal(ks[2], (N_EVEN, D_MODEL, EVEN_IN_COLS), f32) * D_MODEL ** -0.5
    q_norm_even = 1.0 + 0.02 * jax.random.normal(ks[3], (N_EVEN, HEAD_DIM), f32)
    k_norm_even = 1.0 + 0.02 * jax.random.normal(ks[4], (N_EVEN, HEAD_DIM), f32)
    w_out_even = jax.random.normal(ks[5], (N_EVEN, D_INNER, D_MODEL), f32) * D_INNER ** -0.5
    norm_odd = 1.0 + 0.02 * jax.random.normal(ks[6], (N_ODD, D_MODEL), f32)
    w_in_odd = jax.random.normal(ks[7], (N_ODD, D_MODEL, ODD_IN_COLS), f32) * D_MODEL ** -0.5
    lb_logits = 0.1 * jax.random.normal(ks[8], (DEPTH, D_INNER), f32)
    o_norm_odd = 1.0 + 0.02 * jax.random.normal(ks[9], (N_ODD, HEAD_DIM), f32)
    w_out_odd = jax.random.normal(ks[10], (N_ODD, D_INNER, D_MODEL), f32) * D_INNER ** -0.5
    return {"x": x, "norm_even": norm_even, "w_in_even": w_in_even,
            "q_norm_even": q_norm_even, "k_norm_even": k_norm_even, "w_out_even": w_out_even,
            "norm_odd": norm_odd, "w_in_odd": w_in_odd, "lb_logits": lb_logits,
            "o_norm_odd": o_norm_odd, "w_out_odd": w_out_odd}


def reference(x, norm_even, w_in_even, q_norm_even, k_norm_even, w_out_even,
              norm_odd, w_in_odd, lb_logits, o_norm_odd, w_out_odd):
    p = jax.nn.softmax(lb_logits.astype(jnp.float32), axis=0)
    lb_all = jnp.cumsum(p, axis=0) - p[0:1]
    for layer in range(DEPTH):
        j = layer // 2
        if layer % 2 == 0:
            x = even_layer(x, norm_even[j], w_in_even[j], q_norm_even[j], k_norm_even[j], w_out_even[j])
        else:
            x = odd_layer(x, norm_odd[j], w_in_odd[j], lb_all[layer], o_norm_odd[j], w_out_odd[j])
    return x
```

```python
import functools
import math

import numpy as np
import jax
import jax.numpy as jnp
from jax import lax
from jax.experimental import pallas as pl
from jax.experimental.pallas import tpu as pltpu

HEAD_DIM = 128
RMS_EPS = 1e-6
DILATED_GROUPS = ((128, 1), (512, 4), (2048, 16))
DIL_BLOCK = 128
HGRN_CHUNK = 128
HGRN_LEVELS = 7
NEG_BIG = -1e30
VMEM_LIMIT_BYTES = 56 * 1024 * 1024

F32 = jnp.float32
BF16 = jnp.bfloat16
NT_DIMS = (((1,), (1,)), ((), ()))
TN_DIMS = (((0,), (0,)), ((), ()))


def _compiler_params(semantics):
    return pltpu.CompilerParams(dimension_semantics=semantics,
                                vmem_limit_bytes=VMEM_LIMIT_BYTES)


def _rmsnorm_kernel(x_ref, g_ref, o_ref):
    x = x_ref[...]
    ms = jnp.mean(x * x, axis=-1, keepdims=True)
    o_ref[...] = (x * lax.rsqrt(ms + RMS_EPS) * g_ref[...]).astype(o_ref.dtype)


def _rmsnorm(x2d, gain, tm=512):
    m, d = x2d.shape
    return pl.pallas_call(
        _rmsnorm_kernel,
        grid=(m // tm,),
        in_specs=[pl.BlockSpec((tm, d), lambda i: (i, 0)),
                  pl.BlockSpec((1, d), lambda i: (0, 0))],
        out_specs=pl.BlockSpec((tm, d), lambda i: (i, 0)),
        out_shape=jax.ShapeDtypeStruct((m, d), BF16),
        compiler_params=_compiler_params(("parallel",)),
        name="rmsnorm",
    )(x2d, gain.reshape(1, d))


def _proj_kernel(a_ref, w_ref, o_ref):
    o_ref[...] = jnp.dot(a_ref[...], w_ref[...], preferred_element_type=F32).astype(o_ref.dtype)


def _proj_headnorm_kernel(a_ref, w_ref, g_ref, o_ref):
    acc = jnp.dot(a_ref[...], w_ref[...], preferred_element_type=F32)
    for h in range(acc.shape[1] // HEAD_DIM):
        cols = slice(h * HEAD_DIM, (h + 1) * HEAD_DIM)
        blk = acc[:, cols]
        ms = jnp.mean(blk * blk, axis=-1, keepdims=True)
        o_ref[:, cols] = (blk * lax.rsqrt(ms + RMS_EPS) * g_ref[:, cols]).astype(o_ref.dtype)


def _project(a, w, out_dtype, head_gain=None, tm=1024, tn=1024):
    m, k = a.shape
    n = w.shape[1]
    in_specs = [pl.BlockSpec((tm, k), lambda j, i: (i, 0)),
                pl.BlockSpec((k, tn), lambda j, i: (0, j))]
    args = [a, w]
    kern = _proj_kernel
    if head_gain is not None:
        in_specs.append(pl.BlockSpec((1, tn), lambda j, i: (0, j)))
        args.append(head_gain)
        kern = _proj_headnorm_kernel
    return pl.pallas_call(
        kern,
        grid=(n // tn, m // tm),
        in_specs=in_specs,
        out_specs=pl.BlockSpec((tm, tn), lambda j, i: (i, j)),
        out_shape=jax.ShapeDtypeStruct((m, n), out_dtype),
        compiler_params=_compiler_params(("parallel", "parallel")),
        name="project",
    )(*args)


def _out_proj_kernel(n_pairs, *refs):
    x_ref = refs[0]
    o_ref = refs[-1]
    acc = x_ref[...]
    for p in range(n_pairs):
        acc = acc + jnp.dot(refs[1 + 2 * p][...], refs[2 + 2 * p][...], preferred_element_type=F32)
    o_ref[...] = acc


def _out_project(x2d, pairs, tm=512, tn=1024):
    m, n = x2d.shape
    in_specs = [pl.BlockSpec((tm, tn), lambda j, i: (i, j))]
    args = [x2d]
    for a, w in pairs:
        k = a.shape[1]
        in_specs.append(pl.BlockSpec((tm, k), lambda j, i: (i, 0)))
        in_specs.append(pl.BlockSpec((k, tn), lambda j, i: (0, j)))
        args += [a, w]
    return pl.pallas_call(
        functools.partial(_out_proj_kernel, len(pairs)),
        grid=(n // tn, m // tm),
        in_specs=in_specs,
        out_specs=pl.BlockSpec((tm, tn), lambda j, i: (i, j)),
        out_shape=jax.ShapeDtypeStruct((m, n), F32),
        compiler_params=_compiler_params(("parallel", "parallel")),
        name="out_project",
    )(*args)


SB_TILE = 128


def _suffix_sum_matrix():
    t = SB_TILE
    sp = np.arange(t)[:, None]
    s = np.arange(t)[None, :]
    half = np.concatenate([(sp > s).astype(np.float32), np.ones((t, t), np.float32)], axis=1)
    return jnp.asarray(np.concatenate([half, half], axis=0), dtype=BF16)


def _split_bf16(x):
    hi = x.astype(BF16)
    lo = (x - hi.astype(F32)).astype(BF16)
    return jnp.concatenate([hi, lo], axis=1)


def _sb_kernel(q_ref, k_ref, v_ref, g_ref, u_ref, o_ref):
    t = SB_TILE
    qi = pl.program_id(2)
    q = q_ref[...]
    u = u_ref[...]
    scale = 1.0 / math.sqrt(HEAD_DIM)
    row = lax.broadcasted_iota(jnp.int32, (t, t), 0)
    col = lax.broadcasted_iota(jnp.int32, (t, t), 1)
    causal = col < row

    def tile(j, acc, run, diagonal):
        start = pl.multiple_of(j * t, t)
        ks = k_ref[pl.ds(start, t), :]
        vs = v_ref[pl.ds(start, t), :]
        z = lax.dot_general(q, ks, NT_DIMS, preferred_element_type=F32) * scale
        l_raw = jnp.minimum(-z, 0.0) - jnp.log(1.0 + jnp.exp(-jnp.abs(z)))
        l = jnp.where(causal, l_raw, 0.0) if diagonal else l_raw
        sums = jnp.dot(_split_bf16(l), u, preferred_element_type=F32)
        inner = sums[:, :t]
        total = sums[:, t:]
        a = jnp.exp(z + l_raw + inner + run)
        if diagonal:
            a = jnp.where(causal, a, 0.0)
        acc = acc + jnp.dot(a.astype(BF16), vs, preferred_element_type=F32)
        return acc, run + total

    zeros = jnp.zeros((t, HEAD_DIM), F32)
    acc, run = tile(qi, zeros, zeros, True)

    def body(jj, carry):
        return tile(qi - 1 - jj, carry[0], carry[1], False)

    acc, run = lax.fori_loop(0, qi, body, (acc, run))
    gate = g_ref[...].astype(F32)
    o_ref[...] = (acc * (gate * jax.nn.sigmoid(gate))).astype(o_ref.dtype)


def _sb_attention(qkv, gate, n_heads):
    b, s, _ = qkv.shape
    t = SB_TILE
    return pl.pallas_call(
        _sb_kernel,
        grid=(b, n_heads, s // t),
        in_specs=[
            pl.BlockSpec((None, t, HEAD_DIM), lambda bi, h, qi: (bi, qi, h)),
            pl.BlockSpec((None, s, HEAD_DIM), lambda bi, h, qi: (bi, 0, n_heads + h)),
            pl.BlockSpec((None, s, HEAD_DIM), lambda bi, h, qi: (bi, 0, 2 * n_heads + h)),
            pl.BlockSpec((None, t, HEAD_DIM), lambda bi, h, qi: (bi, qi, h)),
            pl.BlockSpec((2 * t, 2 * t), lambda bi, h, qi: (0, 0)),
        ],
        out_specs=pl.BlockSpec((None, t, HEAD_DIM), lambda bi, h, qi: (bi, qi, h)),
        out_shape=jax.ShapeDtypeStruct((b, s, n_heads * HEAD_DIM), BF16),
        compiler_params=_compiler_params(("parallel", "parallel", "arbitrary")),
        name="stickbreak_attention",
    )(qkv, qkv, qkv, gate, _suffix_sum_matrix())


def _dil_kernel(q_ref, k_ref, v_ref, g_ref, o_ref, acc_ref, m_ref, l_ref):
    s_len = q_ref.shape[0]
    blk = DIL_BLOCK
    scale = 1.0 / math.sqrt(HEAD_DIM)
    acc_ref[...] = jnp.zeros_like(acc_ref)
    l_ref[...] = jnp.zeros_like(l_ref)
    m_ref[...] = jnp.full_like(m_ref, NEG_BIG)
    qi = lax.broadcasted_iota(jnp.int32, (blk, 2 * blk), 0)
    kj = lax.broadcasted_iota(jnp.int32, (blk, 2 * blk), 1)
    band = (kj >= qi) & (kj <= qi + blk)
    ones = jnp.ones((2 * blk, HEAD_DIM), BF16)

    for _, r in DILATED_GROUPS:
        nblk = s_len // (r * blk)

        def rows(start, r=r):
            return pl.ds(start, blk) if r == 1 else pl.ds(start, blk, stride=r)

        def body(idx, carry, r=r, nblk=nblk, rows=rows):
            phase = idx // nblk
            n = idx % nblk
            cur = phase + r * blk * n
            prev = phase + r * blk * jnp.maximum(n - 1, 0)
            q = q_ref[rows(cur), :].astype(BF16)
            kc = jnp.concatenate([k_ref[rows(prev), :], k_ref[rows(cur), :]], axis=0).astype(BF16)
            vc = jnp.concatenate([v_ref[rows(prev), :], v_ref[rows(cur), :]], axis=0).astype(BF16)
            sc = lax.dot_general(q, kc, NT_DIMS, preferred_element_type=F32) * scale
            valid = band & ((kj >= blk) | (n > 0))
            sc = jnp.where(valid, sc, NEG_BIG)
            m_old = m_ref[rows(cur), :]
            m_new = jnp.maximum(m_old, jnp.max(sc, axis=-1, keepdims=True))
            p = jnp.concatenate([jnp.exp(sc[:, :blk] - m_new), jnp.exp(sc[:, blk:] - m_new)], axis=1)
            pv = jnp.dot(p.astype(BF16), jnp.concatenate([vc, ones], axis=1),
                         preferred_element_type=F32)
            alpha = jnp.exp(m_old - m_new)
            acc_ref[rows(cur), :] = alpha * acc_ref[rows(cur), :] + pv[:, :HEAD_DIM]
            l_ref[rows(cur), :] = alpha * l_ref[rows(cur), :] + pv[:, HEAD_DIM:]
            m_ref[rows(cur), :] = m_new
            return carry

        lax.fori_loop(0, r * nblk, body, 0)

    def finish(i, carry):
        sl = pl.ds(pl.multiple_of(i * blk, blk), blk)
        gate = g_ref[sl, :].astype(F32)
        o_ref[sl, :] = (acc_ref[sl, :] / l_ref[sl, :] * (gate * jax.nn.sigmoid(gate))).astype(o_ref.dtype)
        return carry

    lax.fori_loop(0, s_len // blk, finish, 0)


def _dilated_attention(qk, v, gate, n_heads, gate_col0):
    b, s, _ = v.shape
    seq = lambda off: pl.BlockSpec((None, s, HEAD_DIM), lambda bi, h: (bi, 0, off + h))
    return pl.pallas_call(
        _dil_kernel,
        grid=(b, n_heads),
        in_specs=[seq(0), seq(n_heads), seq(0), seq(gate_col0)],
        out_specs=seq(0),
        out_shape=jax.ShapeDtypeStruct((b, s, n_heads * HEAD_DIM), BF16),
        scratch_shapes=[pltpu.VMEM((s, HEAD_DIM), F32)] * 3,
        compiler_params=_compiler_params(("parallel", "parallel")),
        name="dilated_attention",
    )(qk, qk, v, gate)


def _hgrn_sum_matrix():
    c = HGRN_CHUNK
    t = np.arange(c)[:, None]
    j = np.arange(c)[None, :]
    blocks = [(j <= t), (j > t)]
    for lev in range(HGRN_LEVELS):
        m = 1 << lev
        mid = (t // (2 * m)) * (2 * m) + m - 1
        second = ((t // m) % 2) == 1
        blocks.append(np.where(second, (j > mid) & (j <= t), (j > t) & (j <= mid)))
    mat = np.concatenate(blocks, axis=0).astype(np.float32)
    return jnp.asarray(np.concatenate([mat, mat], axis=1), dtype=BF16)


def _hgrn_kernel(layer, q_ref, f_ref, i_ref, g_ref, lbl_ref, og_ref, sm_ref, o_ref, st_ref):
    c = HGRN_CHUNK

    @pl.when(pl.program_id(2) == 0)
    def _():
        st_ref[...] = jnp.zeros_like(st_ref)

    lbl = lbl_ref[...]
    e = jnp.exp(lbl - jnp.max(lbl, axis=0, keepdims=True))
    share = e / jnp.sum(e, axis=0, keepdims=True)
    lb = jnp.sum(share[:layer + 1], axis=0, keepdims=True) - share[0:1]

    fp = f_ref[...]
    f = lb + (1.0 - lb) * jax.nn.sigmoid(fp)
    key = (1.0 - lb) * jax.nn.sigmoid(-fp)
    logf = jnp.log(f)
    hi = logf.astype(BF16)
    lo = (logf - hi.astype(F32)).astype(BF16)
    sums = jnp.dot(sm_ref[...], jnp.concatenate([hi, lo], axis=0), preferred_element_type=F32)

    q = q_ref[...].astype(F32)
    inp = i_ref[...]
    row = lax.broadcasted_iota(jnp.int32, (c, c), 0)
    col = lax.broadcasted_iota(jnp.int32, (c, c), 1)

    prefix = sums[0:c]
    state = st_ref[...]
    out = lax.dot_general((q * jnp.exp(prefix)).astype(BF16), state.astype(BF16), NT_DIMS,
                          preferred_element_type=F32)

    attn = jnp.where(row == col,
                     lax.dot_general(q.astype(BF16), key.astype(BF16), NT_DIMS, preferred_element_type=F32),
                     0.0)
    for lev in range(HGRN_LEVELS):
        w = jnp.exp(sums[(2 + lev) * c:(3 + lev) * c])
        is_query = ((row >> lev) & 1) == 1
        qe = jnp.where(is_query, q * w, 0.0).astype(BF16)
        ke = jnp.where(is_query, 0.0, key * w).astype(BF16)
        pair = lax.dot_general(qe, ke, NT_DIMS, preferred_element_type=F32)
        attn = attn + jnp.where((row >> (lev + 1)) == (col >> (lev + 1)), pair, 0.0)
    out = out + jnp.dot(attn.astype(BF16), inp, preferred_element_type=F32)

    key_tail = (key * jnp.exp(sums[c:2 * c])).astype(BF16)
    st_ref[...] = state * jnp.exp(prefix[c - 1:c, :]) + lax.dot_general(
        inp, key_tail, TN_DIMS, preferred_element_type=F32)

    ms = jnp.mean(out * out, axis=-1, keepdims=True)
    gate = g_ref[...].astype(F32)
    o_ref[...] = (out * lax.rsqrt(ms + RMS_EPS) * og_ref[...] * (gate * jax.nn.sigmoid(gate))).astype(o_ref.dtype)


def _hgrn(q, f_pre, inp_gate, lb_logits, o_gain, layer, n_heads):
    b, s, _ = q.shape
    c = HGRN_CHUNK
    depth = lb_logits.shape[0]
    tok = lambda off: pl.BlockSpec((None, c, HEAD_DIM), lambda bi, h, ci: (bi, ci, off + h))
    return pl.pallas_call(
        functools.partial(_hgrn_kernel, layer),
        grid=(b, n_heads, s // c),
        in_specs=[tok(0), tok(0), tok(0), tok(n_heads),
                  pl.BlockSpec((depth, HEAD_DIM), lambda bi, h, ci: (0, h)),
                  pl.BlockSpec((1, HEAD_DIM), lambda bi, h, ci: (0, 0)),
                  pl.BlockSpec(((2 + HGRN_LEVELS) * c, 2 * c), lambda bi, h, ci: (0, 0))],
        out_specs=tok(0),
        out_shape=jax.ShapeDtypeStruct((b, s, n_heads * HEAD_DIM), BF16),
        scratch_shapes=[pltpu.VMEM((HEAD_DIM, HEAD_DIM), F32)],
        compiler_params=_compiler_params(("parallel", "parallel", "arbitrary")),
        name="hgrn2",
    )(q, f_pre, inp_gate, inp_gate, lb_logits, o_gain.reshape(1, HEAD_DIM), _hgrn_sum_matrix())


def _even_layer(x, norm_g, w_in, q_gain, k_gain, w_out):
    b, s, d = x.shape
    d_inner = w_out.shape[0]
    d_half = d_inner // 2
    n_heads = d_half // HEAD_DIM
    m = b * s
    x2d = x.reshape(m, d)
    hdn = _rmsnorm(x2d, norm_g)
    c_sb, c_dil_qk, c_dil_v = 3 * d_half, 5 * d_half, 6 * d_half
    qkv_sb = _project(hdn, w_in[:, :c_sb].astype(BF16), BF16)
    head_gain = jnp.concatenate([jnp.tile(q_gain, n_heads), jnp.tile(k_gain, n_heads)]).reshape(1, 2 * d_half)
    qk_dil = _project(hdn, w_in[:, c_sb:c_dil_qk].astype(BF16), F32, head_gain=head_gain)
    v_dil = _project(hdn, w_in[:, c_dil_qk:c_dil_v].astype(BF16), F32)
    gate = _project(hdn, w_in[:, c_dil_v:].astype(BF16), BF16).reshape(b, s, d_inner)
    mix_sb = _sb_attention(qkv_sb.reshape(b, s, c_sb), gate, n_heads)
    mix_dil = _dilated_attention(qk_dil.reshape(b, s, 2 * d_half), v_dil.reshape(b, s, d_half),
                                 gate, n_heads, n_heads)
    y = _out_project(x2d, [(mix_sb.reshape(m, d_half), w_out[:d_half].astype(BF16)),
                           (mix_dil.reshape(m, d_half), w_out[d_half:].astype(BF16))])
    return y.reshape(b, s, d)


def _odd_layer(x, norm_g, w_in, lb_logits, layer, o_gain, w_out):
    b, s, d = x.shape
    d_inner = w_out.shape[0]
    n_heads = d_inner // HEAD_DIM
    m = b * s
    x2d = x.reshape(m, d)
    hdn = _rmsnorm(x2d, norm_g)
    q = _project(hdn, w_in[:, :d_inner].astype(BF16), BF16)
    f_pre = _project(hdn, w_in[:, d_inner:2 * d_inner].astype(BF16), F32)
    inp_gate = _project(hdn, w_in[:, 2 * d_inner:].astype(BF16), BF16)
    mix = _hgrn(q.reshape(b, s, d_inner), f_pre.reshape(b, s, d_inner),
                inp_gate.reshape(b, s, 2 * d_inner), lb_logits, o_gain, layer, n_heads)
    y = _out_project(x2d, [(mix.reshape(m, d_inner), w_out.astype(BF16))])
    return y.reshape(b, s, d)


def kernel(x, norm_even, w_in_even, q_norm_even, k_norm_even, w_out_even,
           norm_odd, w_in_odd, lb_logits, o_norm_odd, w_out_odd):
    depth = lb_logits.shape[0]
    for layer in range(depth):
        j = layer // 2
        if layer % 2 == 0:
            x = _even_layer(x, norm_even[j], w_in_even[j], q_norm_even[j], k_norm_even[j], w_out_even[j])
        else:
            x = _odd_layer(x, norm_odd[j], w_in_odd[j], lb_logits, layer, o_norm_odd[j], w_out_odd[j])
    return x
```

```python
import functools
import math

import numpy as np
import jax
import jax.numpy as jnp
from jax import lax
from jax.experimental import pallas as pl
from jax.experimental.pallas import tpu as pltpu

HEAD_DIM = 128
RMS_EPS = 1e-6
DILATED_GROUPS = ((128, 1), (512, 4), (2048, 16))
DIL_BLOCK = 128
DIL_UNROLL = 8
SB_TILE = 256
SB_HEADS = 4
SB_LOG2_UNDERFLOW = -150.0
HGRN_CHUNK = 128
HGRN_LEVELS = 7
HGRN_HEADS = 8
NEG_BIG = -1e30
VMEM_LIMIT_BYTES = 56 * 1024 * 1024

F32 = jnp.float32
BF16 = jnp.bfloat16
NT_DIMS = (((1,), (1,)), ((), ()))
TN_DIMS = (((0,), (0,)), ((), ()))


def _compiler_params(semantics):
    return pltpu.CompilerParams(dimension_semantics=semantics,
                                vmem_limit_bytes=VMEM_LIMIT_BYTES)


def _silu(x):
    return x * jax.nn.sigmoid(x)


def _rmsnorm_kernel(x_ref, g_ref, o_ref):
    x = x_ref[...]
    ms = jnp.mean(x * x, axis=-1, keepdims=True)
    o_ref[...] = (x * lax.rsqrt(ms + RMS_EPS) * g_ref[...]).astype(o_ref.dtype)


def _rmsnorm(x2d, gain, tm=512):
    m, d = x2d.shape
    return pl.pallas_call(
        _rmsnorm_kernel,
        grid=(m // tm,),
        in_specs=[pl.BlockSpec((tm, d), lambda i: (i, 0)),
                  pl.BlockSpec((1, d), lambda i: (0, 0))],
        out_specs=pl.BlockSpec((tm, d), lambda i: (i, 0)),
        out_shape=jax.ShapeDtypeStruct((m, d), BF16),
        compiler_params=_compiler_params(("parallel",)),
        name="rmsnorm",
    )(x2d, gain.reshape(1, d))


def _proj_kernel(a_ref, w_ref, o_ref):
    o_ref[...] = jnp.dot(a_ref[...], w_ref[...], preferred_element_type=F32).astype(o_ref.dtype)


def _proj_scaled_kernel(a_ref, w_ref, s_ref, o_ref):
    acc = jnp.dot(a_ref[...], w_ref[...], preferred_element_type=F32)
    o_ref[...] = (acc * s_ref[...]).astype(o_ref.dtype)


def _proj_headnorm_kernel(a_ref, w_ref, g_ref, o_ref):
    acc = jnp.dot(a_ref[...], w_ref[...], preferred_element_type=F32)
    for h in range(acc.shape[1] // HEAD_DIM):
        cols = slice(h * HEAD_DIM, (h + 1) * HEAD_DIM)
        blk = acc[:, cols]
        ms = jnp.mean(blk * blk, axis=-1, keepdims=True)
        o_ref[:, cols] = (blk * lax.rsqrt(ms + RMS_EPS) * g_ref[:, cols]).astype(o_ref.dtype)


def _project(a, w, out_dtype, head_gain=None, col_scale=None, tm=1024, tn=1024):
    assert head_gain is None or col_scale is None
    m, k = a.shape
    n = w.shape[1]
    in_specs = [pl.BlockSpec((tm, k), lambda j, i: (i, 0)),
                pl.BlockSpec((k, tn), lambda j, i: (0, j))]
    args = [a, w]
    kern = _proj_kernel
    if head_gain is not None or col_scale is not None:
        in_specs.append(pl.BlockSpec((1, tn), lambda j, i: (0, j)))
        args.append(head_gain if head_gain is not None else col_scale)
        kern = _proj_headnorm_kernel if head_gain is not None else _proj_scaled_kernel
    return pl.pallas_call(
        kern,
        grid=(n // tn, m // tm),
        in_specs=in_specs,
        out_specs=pl.BlockSpec((tm, tn), lambda j, i: (i, j)),
        out_shape=jax.ShapeDtypeStruct((m, n), out_dtype),
        compiler_params=_compiler_params(("parallel", "parallel")),
        name="project",
    )(*args)


def _out_proj_kernel(n_pairs, *refs):
    x_ref = refs[0]
    o_ref = refs[-1]
    acc = x_ref[...]
    for p in range(n_pairs):
        acc = acc + jnp.dot(refs[1 + 2 * p][...], refs[2 + 2 * p][...], preferred_element_type=F32)
    o_ref[...] = acc


def _out_project(x2d, pairs, tm=512, tn=1024):
    m, n = x2d.shape
    in_specs = [pl.BlockSpec((tm, tn), lambda j, i: (i, j))]
    args = [x2d]
    for a, w in pairs:
        k = a.shape[1]
        in_specs.append(pl.BlockSpec((tm, k), lambda j, i: (i, 0)))
        in_specs.append(pl.BlockSpec((k, tn), lambda j, i: (0, j)))
        args += [a, w]
    return pl.pallas_call(
        functools.partial(_out_proj_kernel, len(pairs)),
        grid=(n // tn, m // tm),
        in_specs=in_specs,
        out_specs=pl.BlockSpec((tm, tn), lambda j, i: (i, j)),
        out_shape=jax.ShapeDtypeStruct((m, n), F32),
        compiler_params=_compiler_params(("parallel", "parallel")),
        name="out_project",
    )(*args)


def _suffix_sum_matrix():
    t = HEAD_DIM
    sp = np.arange(t)[:, None]
    s = np.arange(t)[None, :]
    half = np.concatenate([(sp >= s).astype(np.float32), np.ones((t, t), np.float32)], axis=1)
    return jnp.asarray(np.concatenate([half, half], axis=0), dtype=BF16)


def _split_bf16(x):
    hi = x.astype(BF16)
    lo = (x - hi.astype(F32)).astype(BF16)
    return jnp.concatenate([hi, lo], axis=1)


def _sb_kernel(q_ref, k_ref, v_ref, g_ref, u_ref, o_ref, acc_ref, run_ref):
    t = SB_TILE
    hw = HEAD_DIM
    n_heads = q_ref.shape[1] // hw
    qi = pl.program_id(2)
    u = u_ref[...]
    row = lax.broadcasted_iota(jnp.int32, (t, t), 0)
    col = lax.broadcasted_iota(jnp.int32, (t, t), 1)
    causal = col < row

    def tile(j, diagonal):
        start = pl.multiple_of(j * t, t)
        logits = []
        split_parts = []
        for h in range(n_heads):
            cols = slice(h * hw, (h + 1) * hw)
            z = lax.dot_general(q_ref[:, cols], k_ref[pl.ds(start, t), cols], NT_DIMS,
                                preferred_element_type=F32)
            neg_z = -z
            l = jnp.minimum(neg_z, 0.0) - jnp.log2(1.0 + jnp.exp2(jnp.minimum(z, neg_z)))
            if diagonal:
                l = jnp.where(causal, l, 0.0)
            logits.append(z)
            split_parts += [_split_bf16(l[:, :hw]), _split_bf16(l[:, hw:])]
        sums = jnp.dot(jnp.concatenate(split_parts, axis=0), u, preferred_element_type=F32)
        for h in range(n_heads):
            cols = slice(h * hw, (h + 1) * hw)
            early = sums[(2 * h) * t:(2 * h + 1) * t]
            late = sums[(2 * h + 1) * t:(2 * h + 2) * t]
            after_late = 0.0 if diagonal else run_ref[:, cols]
            after_early = after_late + late[:, hw:]
            z = logits[h]
            a = jnp.exp2(jnp.concatenate([z[:, :hw] + early[:, :hw] + after_early,
                                          z[:, hw:] + late[:, :hw] + after_late], axis=1))
            if diagonal:
                a = jnp.where(causal, a, 0.0)
            pv = jnp.dot(a.astype(BF16), v_ref[pl.ds(start, t), cols], preferred_element_type=F32)
            acc_ref[:, cols] = pv if diagonal else acc_ref[:, cols] + pv
            run_ref[:, cols] = after_early + early[:, hw:]

    tile(qi, True)

    def more(carry):
        return (carry[0] < qi) & (carry[1] > SB_LOG2_UNDERFLOW)

    def body(carry):
        tile(qi - 1 - carry[0], False)
        return carry[0] + 1, jnp.max(run_ref[...])

    lax.while_loop(more, body, (0, jnp.max(run_ref[...])))
    o_ref[...] = (acc_ref[...] * _silu(g_ref[...].astype(F32))).astype(o_ref.dtype)


def _sb_attention(qkv, gate, n_heads):
    b, s, _ = qkv.shape
    t = SB_TILE
    w = SB_HEADS * HEAD_DIM
    groups = n_heads // SB_HEADS
    return pl.pallas_call(
        _sb_kernel,
        grid=(b, groups, s // t),
        in_specs=[
            pl.BlockSpec((None, t, w), lambda bi, h, qi: (bi, qi, h)),
            pl.BlockSpec((None, s, w), lambda bi, h, qi: (bi, 0, groups + h)),
            pl.BlockSpec((None, s, w), lambda bi, h, qi: (bi, 0, 2 * groups + h)),
            pl.BlockSpec((None, t, w), lambda bi, h, qi: (bi, qi, h)),
            pl.BlockSpec((2 * HEAD_DIM, 2 * HEAD_DIM), lambda bi, h, qi: (0, 0)),
        ],
        out_specs=pl.BlockSpec((None, t, w), lambda bi, h, qi: (bi, qi, h)),
        out_shape=jax.ShapeDtypeStruct((b, s, n_heads * HEAD_DIM), BF16),
        scratch_shapes=[pltpu.VMEM((t, w), F32), pltpu.VMEM((t, w), F32)],
        compiler_params=_compiler_params(("parallel", "parallel", "arbitrary")),
        name="stickbreak_attention",
    )(qkv, qkv, qkv, gate, _suffix_sum_matrix())


def _dil_kernel(q_ref, k_ref, v_ref, g_ref, o_ref, acc_ref, m_ref, l_ref):
    s_len = q_ref.shape[0]
    blk = DIL_BLOCK
    scale = 1.0 / math.sqrt(HEAD_DIM)
    acc_ref[...] = jnp.zeros_like(acc_ref)
    l_ref[...] = jnp.zeros_like(l_ref)
    m_ref[...] = jnp.full_like(m_ref, NEG_BIG)
    qi = lax.broadcasted_iota(jnp.int32, (blk, 2 * blk), 0)
    kj = lax.broadcasted_iota(jnp.int32, (blk, 2 * blk), 1)
    band = (kj >= qi) & (kj <= qi + blk)
    ones = jnp.ones((2 * blk, HEAD_DIM), BF16)

    for _, r in DILATED_GROUPS:
        nblk = s_len // (r * blk)

        def rows(start, r=r):
            return pl.ds(start, blk) if r == 1 else pl.ds(start, blk, stride=r)

        def body(it, carry, r=r, nblk=nblk, rows=rows):
            loaded = []
            for u in range(DIL_UNROLL):
                idx = it * DIL_UNROLL + u
                phase = idx // nblk
                n = idx % nblk
                cur = phase + r * blk * n
                prev = phase + r * blk * jnp.maximum(n - 1, 0)
                q = q_ref[rows(cur), :].astype(BF16)
                kc = jnp.concatenate([k_ref[rows(prev), :], k_ref[rows(cur), :]], axis=0).astype(BF16)
                vc = jnp.concatenate([v_ref[rows(prev), :], v_ref[rows(cur), :]], axis=0).astype(BF16)
                loaded.append((n, cur, q, kc, vc, m_ref[rows(cur), :], l_ref[rows(cur), :], acc_ref[rows(cur), :]))
            results = []
            for n, cur, q, kc, vc, m_old, l_old, acc_old in loaded:
                sc = lax.dot_general(q, kc, NT_DIMS, preferred_element_type=F32) * scale
                valid = band & ((kj >= blk) | (n > 0))
                sc = jnp.where(valid, sc, NEG_BIG)
                m_new = jnp.maximum(m_old, jnp.max(sc, axis=-1, keepdims=True))
                p = jnp.concatenate([jnp.exp(sc[:, :blk] - m_new), jnp.exp(sc[:, blk:] - m_new)], axis=1)
                pv = jnp.dot(p.astype(BF16), jnp.concatenate([vc, ones], axis=1),
                             preferred_element_type=F32)
                alpha = jnp.exp(m_old - m_new)
                results.append((cur, m_new, alpha * l_old + pv[:, HEAD_DIM:], alpha * acc_old + pv[:, :HEAD_DIM]))
            for cur, m_new, l_new, acc_new in results:
                m_ref[rows(cur), :] = m_new
                l_ref[rows(cur), :] = l_new
                acc_ref[rows(cur), :] = acc_new
            return carry

        lax.fori_loop(0, (r * nblk) // DIL_UNROLL, body, 0)

    def finish(i, carry):
        sl = pl.ds(pl.multiple_of(i * blk, blk), blk)
        o_ref[sl, :] = (acc_ref[sl, :] / l_ref[sl, :] * _silu(g_ref[sl, :].astype(F32))).astype(o_ref.dtype)
        return carry

    lax.fori_loop(0, s_len // blk, finish, 0)


def _dilated_attention(qk, v, gate, n_heads, gate_col0):
    b, s, _ = v.shape
    seq = lambda off: pl.BlockSpec((None, s, HEAD_DIM), lambda bi, h: (bi, 0, off + h))
    return pl.pallas_call(
        _dil_kernel,
        grid=(b, n_heads),
        in_specs=[seq(0), seq(n_heads), seq(0), seq(gate_col0)],
        out_specs=seq(0),
        out_shape=jax.ShapeDtypeStruct((b, s, n_heads * HEAD_DIM), BF16),
        scratch_shapes=[pltpu.VMEM((s, HEAD_DIM), F32)] * 3,
        compiler_params=_compiler_params(("parallel", "parallel")),
        name="dilated_attention",
    )(qk, qk, v, gate)


def _hgrn_sum_matrix():
    c = HGRN_CHUNK
    t = np.arange(c)[:, None]
    j = np.arange(c)[None, :]
    blocks = [(j <= t), (j > t)]
    for lev in range(HGRN_LEVELS):
        m = 1 << lev
        mid = (t // (2 * m)) * (2 * m) + m - 1
        second = ((t // m) % 2) == 1
        blocks.append(np.where(second, (j > mid) & (j <= t), (j > t) & (j <= mid)))
    mat = np.concatenate(blocks, axis=0).astype(np.float32)
    return jnp.asarray(np.concatenate([mat, mat], axis=1), dtype=BF16)


def _hgrn_kernel(layer, q_ref, f_ref, i_ref, g_ref, lbl_ref, og_ref, sm_ref, o_ref, st_ref):
    c = HGRN_CHUNK
    hw = HEAD_DIM
    n_heads = q_ref.shape[1] // hw

    @pl.when(pl.program_id(2) == 0)
    def _():
        st_ref[...] = jnp.zeros_like(st_ref)

    lbl = lbl_ref[...]
    e = jnp.exp(lbl - jnp.max(lbl, axis=0, keepdims=True))
    share = e / jnp.sum(e, axis=0, keepdims=True)
    lb_all = jnp.sum(share[:layer + 1], axis=0, keepdims=True) - share[0:1]

    row = lax.broadcasted_iota(jnp.int32, (c, c), 0)
    col = lax.broadcasted_iota(jnp.int32, (c, c), 1)
    lower = col < row
    split_bit = row ^ col
    level_masks = [lower & ((split_bit >> lev) == 1) for lev in range(HGRN_LEVELS)]

    keys = []
    splits = []
    for h in range(n_heads):
        cols = slice(h * hw, (h + 1) * hw)
        lb = lb_all[:, cols]
        fp = f_ref[:, cols]
        f = lb + (1.0 - lb) * jax.nn.sigmoid(fp)
        keys.append((1.0 - lb) * jax.nn.sigmoid(-fp))
        log2f = jnp.log2(f)
        hi = log2f.astype(BF16)
        splits.append((hi, (log2f - hi.astype(F32)).astype(BF16)))
    rhs = jnp.concatenate([jnp.concatenate([s[0] for s in splits], axis=1),
                           jnp.concatenate([s[1] for s in splits], axis=1)], axis=0)
    sums_all = jnp.dot(sm_ref[...], rhs, preferred_element_type=F32)

    for h in range(n_heads):
        cols = slice(h * hw, (h + 1) * hw)
        sums = sums_all[:, cols]
        key = keys[h]
        q = q_ref[:, cols].astype(F32)
        inp = i_ref[:, cols]
        prefix = sums[0:c]
        state = st_ref[h]
        out = lax.dot_general((q * jnp.exp2(prefix)).astype(BF16), state.astype(BF16), NT_DIMS,
                              preferred_element_type=F32)
        attn = jnp.where(row == col,
                         lax.dot_general(q.astype(BF16), key.astype(BF16), NT_DIMS,
                                         preferred_element_type=F32), 0.0)
        for lev in range(HGRN_LEVELS):
            w = jnp.exp2(sums[(2 + lev) * c:(3 + lev) * c])
            pair = lax.dot_general((q * w).astype(BF16), (key * w).astype(BF16), NT_DIMS,
                                   preferred_element_type=F32)
            attn = jnp.where(level_masks[lev], pair, attn)
        out = out + jnp.dot(attn.astype(BF16), inp, preferred_element_type=F32)

        key_tail = (key * jnp.exp2(sums[c:2 * c])).astype(BF16)
        st_ref[h] = state * jnp.exp2(prefix[c - 1:c, :]) + lax.dot_general(
            inp, key_tail, TN_DIMS, preferred_element_type=F32)

        ms = jnp.mean(out * out, axis=-1, keepdims=True)
        o_ref[:, cols] = (out * lax.rsqrt(ms + RMS_EPS) * og_ref[...]
                          * _silu(g_ref[:, cols].astype(F32))).astype(o_ref.dtype)


def _hgrn(q, f_pre, inp_gate, lb_logits, o_gain, layer, n_heads):
    b, s, _ = q.shape
    c = HGRN_CHUNK
    w = HGRN_HEADS * HEAD_DIM
    groups = n_heads // HGRN_HEADS
    depth = lb_logits.shape[0]
    tok = lambda off: pl.BlockSpec((None, c, w), lambda bi, h, ci: (bi, ci, off + h))
    return pl.pallas_call(
        functools.partial(_hgrn_kernel, layer),
        grid=(b, groups, s // c),
        in_specs=[tok(0), tok(0), tok(0), tok(groups),
                  pl.BlockSpec((depth, w), lambda bi, h, ci: (0, h)),
                  pl.BlockSpec((1, HEAD_DIM), lambda bi, h, ci: (0, 0)),
                  pl.BlockSpec(((2 + HGRN_LEVELS) * c, 2 * c), lambda bi, h, ci: (0, 0))],
        out_specs=tok(0),
        out_shape=jax.ShapeDtypeStruct((b, s, n_heads * HEAD_DIM), BF16),
        scratch_shapes=[pltpu.VMEM((HGRN_HEADS, HEAD_DIM, HEAD_DIM), F32)],
        compiler_params=_compiler_params(("parallel", "parallel", "arbitrary")),
        name="hgrn2",
    )(q, f_pre, inp_gate, inp_gate, lb_logits, o_gain.reshape(1, HEAD_DIM), _hgrn_sum_matrix())


def _even_layer(x, norm_g, w_in, q_gain, k_gain, w_out):
    b, s, d = x.shape
    d_inner = w_out.shape[0]
    d_half = d_inner // 2
    n_heads = d_half // HEAD_DIM
    m = b * s
    x2d = x.reshape(m, d)
    hdn = _rmsnorm(x2d, norm_g)
    c_sb, c_dil_qk, c_dil_v = 3 * d_half, 5 * d_half, 6 * d_half
    q_scale = math.log2(math.e) / math.sqrt(HEAD_DIM)
    sb_scale = jnp.concatenate([jnp.full((1, d_half), q_scale, F32), jnp.ones((1, 2 * d_half), F32)], axis=1)
    qkv_sb = _project(hdn, w_in[:, :c_sb].astype(BF16), BF16, col_scale=sb_scale)
    head_gain = jnp.concatenate([jnp.tile(q_gain, n_heads), jnp.tile(k_gain, n_heads)]).reshape(1, 2 * d_half)
    qk_dil = _project(hdn, w_in[:, c_sb:c_dil_qk].astype(BF16), F32, head_gain=head_gain)
    v_dil = _project(hdn, w_in[:, c_dil_qk:c_dil_v].astype(BF16), F32)
    gate = _project(hdn, w_in[:, c_dil_v:].astype(BF16), BF16).reshape(b, s, d_inner)
    mix_sb = _sb_attention(qkv_sb.reshape(b, s, c_sb), gate, n_heads)
    mix_dil = _dilated_attention(qk_dil.reshape(b, s, 2 * d_half), v_dil.reshape(b, s, d_half),
                                 gate, n_heads, n_heads)
    y = _out_project(x2d, [(mix_sb.reshape(m, d_half), w_out[:d_half].astype(BF16)),
                           (mix_dil.reshape(m, d_half), w_out[d_half:].astype(BF16))])
    return y.reshape(b, s, d)


def _odd_layer(x, norm_g, w_in, lb_logits, layer, o_gain, w_out):
    b, s, d = x.shape
    d_inner = w_out.shape[0]
    n_heads = d_inner // HEAD_DIM
    m = b * s
    x2d = x.reshape(m, d)
    hdn = _rmsnorm(x2d, norm_g)
    q = _project(hdn, w_in[:, :d_inner].astype(BF16), BF16)
    f_pre = _project(hdn, w_in[:, d_inner:2 * d_inner].astype(BF16), F32)
    inp_gate = _project(hdn, w_in[:, 2 * d_inner:].astype(BF16), BF16)
    mix = _hgrn(q.reshape(b, s, d_inner), f_pre.reshape(b, s, d_inner),
                inp_gate.reshape(b, s, 2 * d_inner), lb_logits, o_gain, layer, n_heads)
    y = _out_project(x2d, [(mix.reshape(m, d_inner), w_out.astype(BF16))])
    return y.reshape(b, s, d)


def kernel(x, norm_even, w_in_even, q_norm_even, k_norm_even, w_out_even,
           norm_odd, w_in_odd, lb_logits, o_norm_odd, w_out_odd):
    depth = lb_logits.shape[0]
    for layer in range(depth):
        j = layer // 2
        if layer % 2 == 0:
            x = _even_layer(x, norm_even[j], w_in_even[j], q_norm_even[j], k_norm_even[j], w_out_even[j])
        else:
            x = _odd_layer(x, norm_odd[j], w_in_odd[j], lb_logits, layer, o_norm_odd[j], w_out_odd[j])
    return x
```

```python
import functools
import math

import numpy as np
import jax
import jax.numpy as jnp
from jax import lax
from jax.experimental import pallas as pl
from jax.experimental.pallas import tpu as pltpu

HEAD_DIM = 128
RMS_EPS = 1e-6
DILATED_GROUPS = ((128, 1), (512, 4), (2048, 16))
DIL_BLOCK = 128
DIL_UNROLL = 8
DIL_INNER_STRIDE = 4
SB_TILE = 256
SB_HEADS = 4
SB_LOG2_UNDERFLOW = -150.0
HGRN_CHUNK = 128
HGRN_LEVELS = 7
HGRN_HEADS = 8
HGRN_DIRECT_LOG2_MAX = 100.0
NEG_BIG = -1e30
VMEM_LIMIT_BYTES = 56 * 1024 * 1024

F32 = jnp.float32
BF16 = jnp.bfloat16
NT_DIMS = (((1,), (1,)), ((), ()))
TN_DIMS = (((0,), (0,)), ((), ()))


def _compiler_params(semantics):
    return pltpu.CompilerParams(dimension_semantics=semantics,
                                vmem_limit_bytes=VMEM_LIMIT_BYTES)


def _silu(x):
    return x * jax.nn.sigmoid(x)


def _rmsnorm_kernel(x_ref, g_ref, o_ref):
    x = x_ref[...]
    ms = jnp.mean(x * x, axis=-1, keepdims=True)
    o_ref[...] = (x * lax.rsqrt(ms + RMS_EPS) * g_ref[...]).astype(o_ref.dtype)


def _rmsnorm(x2d, gain, tm=512):
    m, d = x2d.shape
    return pl.pallas_call(
        _rmsnorm_kernel,
        grid=(m // tm,),
        in_specs=[pl.BlockSpec((tm, d), lambda i: (i, 0)),
                  pl.BlockSpec((1, d), lambda i: (0, 0))],
        out_specs=pl.BlockSpec((tm, d), lambda i: (i, 0)),
        out_shape=jax.ShapeDtypeStruct((m, d), BF16),
        compiler_params=_compiler_params(("parallel",)),
        name="rmsnorm",
    )(x2d, gain.reshape(1, d))


def _resident_weight(w_ref, wb_ref):
    @pl.when(pl.program_id(1) == 0)
    def _():
        wb_ref[...] = w_ref[...].astype(BF16)
    return wb_ref[...]


def _proj_kernel(a_ref, w_ref, o_ref, wb_ref):
    w = _resident_weight(w_ref, wb_ref)
    o_ref[...] = jnp.dot(a_ref[...], w, preferred_element_type=F32).astype(o_ref.dtype)


def _proj_scaled_kernel(a_ref, w_ref, s_ref, o_ref, wb_ref):
    w = _resident_weight(w_ref, wb_ref)
    acc = jnp.dot(a_ref[...], w, preferred_element_type=F32)
    o_ref[...] = (acc * s_ref[...]).astype(o_ref.dtype)


def _proj_headnorm_kernel(a_ref, w_ref, g_ref, o_ref, wb_ref):
    w = _resident_weight(w_ref, wb_ref)
    acc = jnp.dot(a_ref[...], w, preferred_element_type=F32)
    for h in range(acc.shape[1] // HEAD_DIM):
        cols = slice(h * HEAD_DIM, (h + 1) * HEAD_DIM)
        blk = acc[:, cols]
        ms = jnp.mean(blk * blk, axis=-1, keepdims=True)
        o_ref[:, cols] = (blk * lax.rsqrt(ms + RMS_EPS) * g_ref[:, cols]).astype(o_ref.dtype)


def _project(a, w, col0, n, out_dtype, head_gain=None, col_scale=None, tm=1024, tn=1024):
    assert head_gain is None or col_scale is None
    assert col0 % tn == 0 and n % tn == 0
    m, k = a.shape
    jt0 = col0 // tn
    in_specs = [pl.BlockSpec((tm, k), lambda j, i: (i, 0)),
                pl.BlockSpec((k, tn), lambda j, i: (0, jt0 + j))]
    args = [a, w]
    kern = _proj_kernel
    if head_gain is not None or col_scale is not None:
        in_specs.append(pl.BlockSpec((1, tn), lambda j, i: (0, j)))
        args.append(head_gain if head_gain is not None else col_scale)
        kern = _proj_headnorm_kernel if head_gain is not None else _proj_scaled_kernel
    return pl.pallas_call(
        kern,
        grid=(n // tn, m // tm),
        in_specs=in_specs,
        out_specs=pl.BlockSpec((tm, tn), lambda j, i: (i, j)),
        out_shape=jax.ShapeDtypeStruct((m, n), out_dtype),
        scratch_shapes=[pltpu.VMEM((k, tn), BF16)],
        compiler_params=_compiler_params(("parallel", "arbitrary")),
        name="project",
    )(*args)


def _out_proj_kernel(n_pairs, *refs):
    x_ref = refs[0]
    o_ref = refs[-1]
    acc = x_ref[...]
    for p in range(n_pairs):
        acc = acc + jnp.dot(refs[1 + 2 * p][...], refs[2 + 2 * p][...], preferred_element_type=F32)
    o_ref[...] = acc


def _out_project(x2d, pairs, tm=512, tn=1024):
    m, n = x2d.shape
    in_specs = [pl.BlockSpec((tm, tn), lambda j, i: (i, j))]
    args = [x2d]
    for a, w in pairs:
        k = a.shape[1]
        in_specs.append(pl.BlockSpec((tm, k), lambda j, i: (i, 0)))
        in_specs.append(pl.BlockSpec((k, tn), lambda j, i: (0, j)))
        args += [a, w]
    return pl.pallas_call(
        functools.partial(_out_proj_kernel, len(pairs)),
        grid=(n // tn, m // tm),
        in_specs=in_specs,
        out_specs=pl.BlockSpec((tm, tn), lambda j, i: (i, j)),
        out_shape=jax.ShapeDtypeStruct((m, n), F32),
        compiler_params=_compiler_params(("parallel", "parallel")),
        name="out_project",
    )(*args)


def _suffix_sum_matrix():
    t = HEAD_DIM
    sp = np.arange(t)[:, None]
    s = np.arange(t)[None, :]
    half = np.concatenate([(sp >= s).astype(np.float32), np.ones((t, t), np.float32)], axis=1)
    return jnp.asarray(np.concatenate([half, half], axis=0), dtype=BF16)


def _split_bf16(x):
    hi = x.astype(BF16)
    lo = (x - hi.astype(F32)).astype(BF16)
    return jnp.concatenate([hi, lo], axis=1)


def _sb_kernel(q_ref, k_ref, v_ref, g_ref, u_ref, o_ref, acc_ref, run_ref):
    t = SB_TILE
    hw = HEAD_DIM
    n_heads = q_ref.shape[1] // hw
    qi = pl.program_id(2)
    u = u_ref[...]
    row = lax.broadcasted_iota(jnp.int32, (t, t), 0)
    col = lax.broadcasted_iota(jnp.int32, (t, t), 1)
    causal = col < row

    def tile(j, diagonal):
        start = pl.multiple_of(j * t, t)
        logits = []
        split_parts = []
        for h in range(n_heads):
            cols = slice(h * hw, (h + 1) * hw)
            z = lax.dot_general(q_ref[:, cols], k_ref[pl.ds(start, t), cols], NT_DIMS,
                                preferred_element_type=F32)
            neg_z = -z
            l = jnp.minimum(neg_z, 0.0) - jnp.log2(1.0 + jnp.exp2(jnp.minimum(z, neg_z)))
            if diagonal:
                l = jnp.where(causal, l, 0.0)
            logits.append(z)
            split_parts += [_split_bf16(l[:, :hw]), _split_bf16(l[:, hw:])]
        sums = jnp.dot(jnp.concatenate(split_parts, axis=0), u, preferred_element_type=F32)
        for h in range(n_heads):
            cols = slice(h * hw, (h + 1) * hw)
            early = sums[(2 * h) * t:(2 * h + 1) * t]
            late = sums[(2 * h + 1) * t:(2 * h + 2) * t]
            after_late = 0.0 if diagonal else run_ref[:, cols]
            after_early = after_late + late[:, hw:]
            z = logits[h]
            a = jnp.exp2(jnp.concatenate([z[:, :hw] + early[:, :hw] + after_early,
                                          z[:, hw:] + late[:, :hw] + after_late], axis=1))
            if diagonal:
                a = jnp.where(causal, a, 0.0)
            pv = jnp.dot(a.astype(BF16), v_ref[pl.ds(start, t), cols], preferred_element_type=F32)
            acc_ref[:, cols] = pv if diagonal else acc_ref[:, cols] + pv
            run_ref[:, cols] = after_early + early[:, hw:]

    tile(qi, True)

    def more(carry):
        return (carry[0] < qi) & (carry[1] > SB_LOG2_UNDERFLOW)

    def body(carry):
        tile(qi - 1 - carry[0], False)
        return carry[0] + 1, jnp.max(run_ref[...])

    lax.while_loop(more, body, (0, jnp.max(run_ref[...])))
    o_ref[...] = (acc_ref[...] * _silu(g_ref[...].astype(F32))).astype(o_ref.dtype)


def _sb_attention(qkv, gate, n_heads):
    b, s, _ = qkv.shape
    t = SB_TILE
    w = SB_HEADS * HEAD_DIM
    groups = n_heads // SB_HEADS
    return pl.pallas_call(
        _sb_kernel,
        grid=(b, groups, s // t),
        in_specs=[
            pl.BlockSpec((None, t, w), lambda bi, h, qi: (bi, qi, h)),
            pl.BlockSpec((None, s, w), lambda bi, h, qi: (bi, 0, groups + h)),
            pl.BlockSpec((None, s, w), lambda bi, h, qi: (bi, 0, 2 * groups + h)),
            pl.BlockSpec((None, t, w), lambda bi, h, qi: (bi, qi, h)),
            pl.BlockSpec((2 * HEAD_DIM, 2 * HEAD_DIM), lambda bi, h, qi: (0, 0)),
        ],
        out_specs=pl.BlockSpec((None, t, w), lambda bi, h, qi: (bi, qi, h)),
        out_shape=jax.ShapeDtypeStruct((b, s, n_heads * HEAD_DIM), BF16),
        scratch_shapes=[pltpu.VMEM((t, w), F32), pltpu.VMEM((t, w), F32)],
        compiler_params=_compiler_params(("parallel", "parallel", "arbitrary")),
        name="stickbreak_attention",
    )(qkv, qkv, qkv, gate, _suffix_sum_matrix())


def _dil_kernel(q_ref, k_ref, v_ref, g_ref, o_ref, qs_ref, ks_ref, vs_ref, ms_ref, ls_ref, as_ref):
    s_len = q_ref.shape[0]
    blk = DIL_BLOCK
    inner = DIL_INNER_STRIDE
    outer = DILATED_GROUPS[2][1] // inner
    assert DILATED_GROUPS[0][1] == 1 and DILATED_GROUPS[1][1] == inner and outer == inner
    part = s_len // inner
    n_mid = part // blk
    n_wide = part // (outer * blk)
    scale = 1.0 / math.sqrt(HEAD_DIM)
    qi = lax.broadcasted_iota(jnp.int32, (blk, 2 * blk), 0)
    kj = lax.broadcasted_iota(jnp.int32, (blk, 2 * blk), 1)
    band = (kj >= qi) & (kj <= qi + blk)
    band_first = (lax.broadcasted_iota(jnp.int32, (blk, blk), 1)
                  <= lax.broadcasted_iota(jnp.int32, (blk, blk), 0))

    def scores(q, keys, valid):
        sc = lax.dot_general(q.astype(BF16), keys.astype(BF16), NT_DIMS, preferred_element_type=F32)
        return jnp.where(valid, sc, NEG_BIG)

    def weigh(sc, m_new, vals):
        p = jnp.concatenate([jnp.exp(sc[:, i:i + blk] - m_new) for i in range(0, sc.shape[1], blk)], axis=1)
        pv = jnp.dot(p.astype(BF16),
                     jnp.concatenate([vals.astype(BF16), jnp.ones(vals.shape, BF16)], axis=1),
                     preferred_element_type=F32)
        return pv[:, :HEAD_DIM], pv[:, HEAD_DIM:]

    def update(sc, vals, old):
        m_old, l_old, acc_old = old
        m_new = jnp.maximum(m_old, jnp.max(sc, axis=-1, keepdims=True))
        num, den = weigh(sc, m_new, vals)
        alpha = jnp.exp(m_old - m_new)
        return m_new, alpha * l_old + den, alpha * acc_old + num

    def split(i, carry):
        for p in range(inner):
            src = pl.ds(p + inner * blk * i, blk, stride=inner)
            dst = pl.ds(pl.multiple_of(p * part + blk * i, blk), blk)
            qs_ref[dst, :] = q_ref[src, :] * scale
            ks_ref[dst, :] = k_ref[src, :]
            vs_ref[dst, :] = v_ref[src, :]
        return carry

    lax.fori_loop(0, part // blk, split, 0)

    def mid_group(p, carry):
        base = pl.multiple_of(p * part, blk)
        for n in range(n_mid):
            cur = pl.ds(base + n * blk, blk)
            if n == 0:
                sc = scores(qs_ref[cur, :], ks_ref[cur, :], band_first)
                vals = vs_ref[cur, :]
            else:
                both = pl.ds(base + (n - 1) * blk, 2 * blk)
                sc = scores(qs_ref[cur, :], ks_ref[both, :], band)
                vals = vs_ref[both, :]
            m_new = jnp.broadcast_to(jnp.max(sc, axis=-1, keepdims=True), (blk, HEAD_DIM))
            num, den = weigh(sc, m_new, vals)
            ms_ref[cur, :] = m_new
            ls_ref[cur, :] = den
            as_ref[cur, :] = num
        return carry

    lax.fori_loop(0, inner, mid_group, 0)

    def wide_group(p, carry):
        base = p * part
        loaded = []
        for ph in range(outer):
            for n in range(n_wide):
                cur = pl.ds(base + ph + outer * blk * n, blk, stride=outer)
                if n == 0:
                    sc = scores(qs_ref[cur, :], ks_ref[cur, :], band_first)
                    vals = vs_ref[cur, :]
                else:
                    prev = pl.ds(base + ph + outer * blk * (n - 1), blk, stride=outer)
                    sc = scores(qs_ref[cur, :], jnp.concatenate([ks_ref[prev, :], ks_ref[cur, :]], axis=0), band)
                    vals = jnp.concatenate([vs_ref[prev, :], vs_ref[cur, :]], axis=0)
                loaded.append((cur, sc, vals, (ms_ref[cur, :], ls_ref[cur, :], as_ref[cur, :])))
        results = [(cur,) + update(sc, vals, old) for cur, sc, vals, old in loaded]
        for cur, m_new, l_new, acc_new in results:
            ms_ref[cur, :] = m_new
            ls_ref[cur, :] = l_new
            as_ref[cur, :] = acc_new
        return carry

    lax.fori_loop(0, inner, wide_group, 0)

    m_ref, l_ref, acc_ref = qs_ref, ks_ref, vs_ref

    def merge(i, carry):
        for p in range(inner):
            src = pl.ds(pl.multiple_of(p * part + blk * i, blk), blk)
            dst = pl.ds(p + inner * blk * i, blk, stride=inner)
            m_ref[dst, :] = ms_ref[src, :]
            l_ref[dst, :] = ls_ref[src, :]
            acc_ref[dst, :] = as_ref[src, :]
        return carry

    lax.fori_loop(0, part // blk, merge, 0)

    def dense_group(it, carry):
        loaded = []
        for u in range(DIL_UNROLL):
            n = it * DIL_UNROLL + u
            cur = pl.ds(pl.multiple_of(n * blk, blk), blk)
            prev = pl.ds(pl.multiple_of(jnp.maximum(n - 1, 0) * blk, blk), blk)
            keys = jnp.concatenate([k_ref[prev, :], k_ref[cur, :]], axis=0)
            vals = jnp.concatenate([v_ref[prev, :], v_ref[cur, :]], axis=0)
            sc = scores(q_ref[cur, :] * scale, keys, band & ((kj >= blk) | (n > 0)))
            loaded.append((cur, sc, vals, (m_ref[cur, :], l_ref[cur, :], acc_ref[cur, :])))
        for cur, sc, vals, old in loaded:
            _, l_new, acc_new = update(sc, vals, old)
            o_ref[cur, :] = (acc_new / l_new * _silu(g_ref[cur, :].astype(F32))).astype(o_ref.dtype)
        return carry

    lax.fori_loop(0, s_len // (blk * DIL_UNROLL), dense_group, 0)


def _dilated_attention(qk, v, gate, n_heads, gate_col0):
    b, s, _ = v.shape
    seq = lambda off: pl.BlockSpec((None, s, HEAD_DIM), lambda bi, h: (bi, 0, off + h))
    return pl.pallas_call(
        _dil_kernel,
        grid=(b, n_heads),
        in_specs=[seq(0), seq(n_heads), seq(0), seq(gate_col0)],
        out_specs=seq(0),
        out_shape=jax.ShapeDtypeStruct((b, s, n_heads * HEAD_DIM), BF16),
        scratch_shapes=[pltpu.VMEM((s, HEAD_DIM), F32)] * 6,
        compiler_params=_compiler_params(("parallel", "parallel")),
        name="dilated_attention",
    )(qk, qk, v, gate)


def _hgrn_sum_matrix():
    c = HGRN_CHUNK
    t = np.arange(c)[:, None]
    j = np.arange(c)[None, :]
    blocks = [(j <= t), (j > t)]
    for lev in range(HGRN_LEVELS):
        m = 1 << lev
        mid = (t // (2 * m)) * (2 * m) + m - 1
        second = ((t // m) % 2) == 1
        blocks.append(np.where(second, (j > mid) & (j <= t), (j > t) & (j <= mid)))
    mat = np.concatenate(blocks, axis=0).astype(np.float32)
    return jnp.asarray(np.concatenate([mat, mat], axis=1), dtype=BF16)


def _hgrn_kernel(layer, q_ref, f_ref, i_ref, g_ref, lbl_ref, og_ref, sm_ref, o_ref, st_ref):
    c = HGRN_CHUNK
    hw = HEAD_DIM
    n_heads = q_ref.shape[1] // hw

    @pl.when(pl.program_id(2) == 0)
    def _():
        st_ref[...] = jnp.zeros_like(st_ref)

    lbl = lbl_ref[...]
    e = jnp.exp(lbl - jnp.max(lbl, axis=0, keepdims=True))
    share = e / jnp.sum(e, axis=0, keepdims=True)
    lb_all = jnp.sum(share[:layer + 1], axis=0, keepdims=True) - share[0:1]

    row = lax.broadcasted_iota(jnp.int32, (c, c), 0)
    col = lax.broadcasted_iota(jnp.int32, (c, c), 1)

    keys = []
    his = []
    los = []
    for h in range(n_heads):
        cols = slice(h * hw, (h + 1) * hw)
        lb = lb_all[:, cols]
        f = lb + (1.0 - lb) * jax.nn.sigmoid(f_ref[:, cols])
        keys.append(1.0 - f)
        log2f = jnp.log2(f)
        hi = log2f.astype(BF16)
        his.append(hi)
        los.append((log2f - hi.astype(F32)).astype(BF16))
    rhs = jnp.concatenate([jnp.concatenate(his, axis=1), jnp.concatenate(los, axis=1)], axis=0)
    prefix_all = jnp.dot(sm_ref[0:c, :], rhs, preferred_element_type=F32)
    direct_ok = jnp.min(prefix_all[c - 1:c, :]) >= -HGRN_DIRECT_LOG2_MAX

    def finish(h, out, state, inp, key_tail, last_decay):
        cols = slice(h * hw, (h + 1) * hw)
        st_ref[h] = state * last_decay + lax.dot_general(inp, key_tail, TN_DIMS, preferred_element_type=F32)
        ms = jnp.mean(out * out, axis=-1, keepdims=True)
        o_ref[:, cols] = (out * lax.rsqrt(ms + RMS_EPS) * og_ref[...]
                          * _silu(g_ref[:, cols].astype(F32))).astype(o_ref.dtype)

    @pl.when(direct_ok)
    def _():
        keep = col <= row
        for h in range(n_heads):
            cols = slice(h * hw, (h + 1) * hw)
            prefix = prefix_all[:, cols]
            grow = jnp.exp2(-prefix)
            q_dec = (q_ref[:, cols].astype(F32) * jnp.exp2(prefix)).astype(BF16)
            key_grow = keys[h] * grow
            inp = i_ref[:, cols]
            state = st_ref[h]
            attn = jnp.where(keep, lax.dot_general(q_dec, key_grow.astype(BF16), NT_DIMS,
                                                   preferred_element_type=F32), 0.0)
            out = (lax.dot_general(q_dec, state.astype(BF16), NT_DIMS, preferred_element_type=F32)
                   + jnp.dot(attn.astype(BF16), inp, preferred_element_type=F32))
            last_decay = jnp.exp2(prefix[c - 1:c, :])
            finish(h, out, state, inp, (key_grow * last_decay).astype(BF16), last_decay)

    @pl.when(jnp.logical_not(direct_ok))
    def _():
        lower = col < row
        split_bit = row ^ col
        level_masks = [lower & ((split_bit >> lev) == 1) for lev in range(HGRN_LEVELS)]
        sums_all = jnp.dot(sm_ref[c:, :], rhs, preferred_element_type=F32)
        for h in range(n_heads):
            cols = slice(h * hw, (h + 1) * hw)
            sums = sums_all[:, cols]
            prefix = prefix_all[:, cols]
            key = keys[h]
            q = q_ref[:, cols].astype(F32)
            inp = i_ref[:, cols]
            state = st_ref[h]
            out = lax.dot_general((q * jnp.exp2(prefix)).astype(BF16), state.astype(BF16), NT_DIMS,
                                  preferred_element_type=F32)
            attn = jnp.where(row == col,
                             lax.dot_general(q.astype(BF16), key.astype(BF16), NT_DIMS,
                                             preferred_element_type=F32), 0.0)
            for lev in range(HGRN_LEVELS):
                w = jnp.exp2(sums[(1 + lev) * c:(2 + lev) * c])
                pair = lax.dot_general((q * w).astype(BF16), (key * w).astype(BF16), NT_DIMS,
                                       preferred_element_type=F32)
                attn = jnp.where(level_masks[lev], pair, attn)
            out = out + jnp.dot(attn.astype(BF16), inp, preferred_element_type=F32)
            finish(h, out, state, inp, (key * jnp.exp2(sums[0:c])).astype(BF16),
                   jnp.exp2(prefix[c - 1:c, :]))


def _hgrn(q, f_pre, inp_gate, lb_logits, o_gain, layer, n_heads):
    b, s, _ = q.shape
    c = HGRN_CHUNK
    w = HGRN_HEADS * HEAD_DIM
    groups = n_heads // HGRN_HEADS
    depth = lb_logits.shape[0]
    tok = lambda off: pl.BlockSpec((None, c, w), lambda bi, h, ci: (bi, ci, off + h))
    return pl.pallas_call(
        functools.partial(_hgrn_kernel, layer),
        grid=(b, groups, s // c),
        in_specs=[tok(0), tok(0), tok(0), tok(groups),
                  pl.BlockSpec((depth, w), lambda bi, h, ci: (0, h)),
                  pl.BlockSpec((1, HEAD_DIM), lambda bi, h, ci: (0, 0)),
                  pl.BlockSpec(((2 + HGRN_LEVELS) * c, 2 * c), lambda bi, h, ci: (0, 0))],
        out_specs=tok(0),
        out_shape=jax.ShapeDtypeStruct((b, s, n_heads * HEAD_DIM), BF16),
        scratch_shapes=[pltpu.VMEM((HGRN_HEADS, HEAD_DIM, HEAD_DIM), F32)],
        compiler_params=_compiler_params(("parallel", "parallel", "arbitrary")),
        name="hgrn2",
    )(q, f_pre, inp_gate, inp_gate, lb_logits, o_gain.reshape(1, HEAD_DIM), _hgrn_sum_matrix())


def _even_layer(x, norm_g, w_in, q_gain, k_gain, w_out):
    b, s, d = x.shape
    d_inner = w_out.shape[0]
    d_half = d_inner // 2
    n_heads = d_half // HEAD_DIM
    m = b * s
    x2d = x.reshape(m, d)
    hdn = _rmsnorm(x2d, norm_g)
    c_sb, c_dil_qk, c_dil_v = 3 * d_half, 5 * d_half, 6 * d_half
    q_scale = math.log2(math.e) / math.sqrt(HEAD_DIM)
    sb_scale = jnp.concatenate([jnp.full((1, d_half), q_scale, F32), jnp.ones((1, 2 * d_half), F32)], axis=1)
    qkv_sb = _project(hdn, w_in, 0, c_sb, BF16, col_scale=sb_scale)
    head_gain = jnp.concatenate([jnp.tile(q_gain, n_heads), jnp.tile(k_gain, n_heads)]).reshape(1, 2 * d_half)
    qk_dil = _project(hdn, w_in, c_sb, c_dil_qk - c_sb, F32, head_gain=head_gain)
    v_dil = _project(hdn, w_in, c_dil_qk, c_dil_v - c_dil_qk, F32)
    gate = _project(hdn, w_in, c_dil_v, d_inner, BF16).reshape(b, s, d_inner)
    mix_sb = _sb_attention(qkv_sb.reshape(b, s, c_sb), gate, n_heads)
    mix_dil = _dilated_attention(qk_dil.reshape(b, s, 2 * d_half), v_dil.reshape(b, s, d_half),
                                 gate, n_heads, n_heads)
    y = _out_project(x2d, [(mix_sb.reshape(m, d_half), w_out[:d_half].astype(BF16)),
                           (mix_dil.reshape(m, d_half), w_out[d_half:].astype(BF16))])
    return y.reshape(b, s, d)


def _odd_layer(x, norm_g, w_in, lb_logits, layer, o_gain, w_out):
    b, s, d = x.shape
    d_inner = w_out.shape[0]
    n_heads = d_inner // HEAD_DIM
    m = b * s
    x2d = x.reshape(m, d)
    hdn = _rmsnorm(x2d, norm_g)
    q = _project(hdn, w_in, 0, d_inner, BF16)
    f_pre = _project(hdn, w_in, d_inner, d_inner, F32)
    inp_gate = _project(hdn, w_in, 2 * d_inner, 2 * d_inner, BF16)
    mix = _hgrn(q.reshape(b, s, d_inner), f_pre.reshape(b, s, d_inner),
                inp_gate.reshape(b, s, 2 * d_inner), lb_logits, o_gain, layer, n_heads)
    y = _out_project(x2d, [(mix.reshape(m, d_inner), w_out.astype(BF16))])
    return y.reshape(b, s, d)


def kernel(x, norm_even, w_in_even, q_norm_even, k_norm_even, w_out_even,
           norm_odd, w_in_odd, lb_logits, o_norm_odd, w_out_odd):
    depth = lb_logits.shape[0]
    for layer in range(depth):
        j = layer // 2
        if layer % 2 == 0:
            x = _even_layer(x, norm_even[j], w_in_even[j], q_norm_even[j], k_norm_even[j], w_out_even[j])
        else:
            x = _odd_layer(x, norm_odd[j], w_in_odd[j], lb_logits, layer, o_norm_odd[j], w_out_odd[j])
    return x
```

```python
import functools
import math

import numpy as np
import jax
import jax.numpy as jnp
from jax import lax
from jax.experimental import pallas as pl
from jax.experimental.pallas import tpu as pltpu

HEAD_DIM = 128
RMS_EPS = 1e-6
DILATED_GROUPS = ((128, 1), (512, 4), (2048, 16))
DIL_BLOCK = 128
DIL_UNROLL = 8
DIL_INNER_STRIDE = 4
SB_TILE = 256
SB_HEADS = 8
SB_LOG2_UNDERFLOW = -150.0
HGRN_CHUNK = 128
HGRN_LEVELS = 7
HGRN_HEADS = 32
HGRN_DIRECT_LOG2_MAX = 100.0
NEG_BIG = -1e30
VMEM_LIMIT_BYTES = 56 * 1024 * 1024

F32 = jnp.float32
BF16 = jnp.bfloat16
NT_DIMS = (((1,), (1,)), ((), ()))
TN_DIMS = (((0,), (0,)), ((), ()))


def _compiler_params(semantics):
    return pltpu.CompilerParams(dimension_semantics=semantics,
                                vmem_limit_bytes=VMEM_LIMIT_BYTES)


def _silu(x):
    return x * jax.nn.sigmoid(x)


def _rmsnorm_kernel(x_ref, g_ref, o_ref):
    x = x_ref[...]
    ms = jnp.mean(x * x, axis=-1, keepdims=True)
    o_ref[...] = (x * lax.rsqrt(ms + RMS_EPS) * g_ref[...]).astype(o_ref.dtype)


def _rmsnorm(x2d, gain, tm=512):
    m, d = x2d.shape
    return pl.pallas_call(
        _rmsnorm_kernel,
        grid=(m // tm,),
        in_specs=[pl.BlockSpec((tm, d), lambda i: (i, 0)),
                  pl.BlockSpec((1, d), lambda i: (0, 0))],
        out_specs=pl.BlockSpec((tm, d), lambda i: (i, 0)),
        out_shape=jax.ShapeDtypeStruct((m, d), BF16),
        compiler_params=_compiler_params(("parallel",)),
        name="rmsnorm",
    )(x2d, gain.reshape(1, d))


def _resident_weight(w_ref, wb_ref):
    @pl.when(pl.program_id(1) == 0)
    def _():
        wb_ref[...] = w_ref[...].astype(BF16)
    return wb_ref[...]


def _proj_kernel(a_ref, w_ref, o_ref, wb_ref):
    w = _resident_weight(w_ref, wb_ref)
    o_ref[...] = jnp.dot(a_ref[...], w, preferred_element_type=F32).astype(o_ref.dtype)


def _proj_scaled_kernel(a_ref, w_ref, s_ref, o_ref, wb_ref):
    w = _resident_weight(w_ref, wb_ref)
    acc = jnp.dot(a_ref[...], w, preferred_element_type=F32)
    o_ref[...] = (acc * s_ref[...]).astype(o_ref.dtype)


def _proj_headnorm_kernel(a_ref, w_ref, g_ref, o_ref, wb_ref):
    w = _resident_weight(w_ref, wb_ref)
    acc = jnp.dot(a_ref[...], w, preferred_element_type=F32)
    for h in range(acc.shape[1] // HEAD_DIM):
        cols = slice(h * HEAD_DIM, (h + 1) * HEAD_DIM)
        blk = acc[:, cols]
        ms = jnp.mean(blk * blk, axis=-1, keepdims=True)
        o_ref[:, cols] = (blk * lax.rsqrt(ms + RMS_EPS) * g_ref[:, cols]).astype(o_ref.dtype)


def _project(a, w, col0, n, out_dtype, head_gain=None, col_scale=None, tm=1024, tn=1024):
    assert head_gain is None or col_scale is None
    assert col0 % tn == 0 and n % tn == 0
    m, k = a.shape
    jt0 = col0 // tn
    in_specs = [pl.BlockSpec((tm, k), lambda j, i: (i, 0)),
                pl.BlockSpec((k, tn), lambda j, i: (0, jt0 + j))]
    args = [a, w]
    kern = _proj_kernel
    if head_gain is not None or col_scale is not None:
        in_specs.append(pl.BlockSpec((1, tn), lambda j, i: (0, j)))
        args.append(head_gain if head_gain is not None else col_scale)
        kern = _proj_headnorm_kernel if head_gain is not None else _proj_scaled_kernel
    return pl.pallas_call(
        kern,
        grid=(n // tn, m // tm),
        in_specs=in_specs,
        out_specs=pl.BlockSpec((tm, tn), lambda j, i: (i, j)),
        out_shape=jax.ShapeDtypeStruct((m, n), out_dtype),
        scratch_shapes=[pltpu.VMEM((k, tn), BF16)],
        compiler_params=_compiler_params(("parallel", "arbitrary")),
        name="project",
    )(*args)


def _out_proj_kernel(n_pairs, *refs):
    x_ref = refs[0]
    o_ref = refs[-1]
    acc = x_ref[...]
    for p in range(n_pairs):
        acc = acc + jnp.dot(refs[1 + 2 * p][...], refs[2 + 2 * p][...], preferred_element_type=F32)
    o_ref[...] = acc


def _out_project(x2d, pairs, tm=512, tn=1024):
    m, n = x2d.shape
    in_specs = [pl.BlockSpec((tm, tn), lambda j, i: (i, j))]
    args = [x2d]
    for a, w in pairs:
        k = a.shape[1]
        in_specs.append(pl.BlockSpec((tm, k), lambda j, i: (i, 0)))
        in_specs.append(pl.BlockSpec((k, tn), lambda j, i: (0, j)))
        args += [a, w]
    return pl.pallas_call(
        functools.partial(_out_proj_kernel, len(pairs)),
        grid=(n // tn, m // tm),
        in_specs=in_specs,
        out_specs=pl.BlockSpec((tm, tn), lambda j, i: (i, j)),
        out_shape=jax.ShapeDtypeStruct((m, n), F32),
        compiler_params=_compiler_params(("parallel", "parallel")),
        name="out_project",
    )(*args)


def _suffix_sum_matrix():
    t = HEAD_DIM
    sp = np.arange(t)[:, None]
    s = np.arange(t)[None, :]
    half = np.concatenate([(sp >= s).astype(np.float32), np.ones((t, t), np.float32)], axis=1)
    return jnp.asarray(np.concatenate([half, half], axis=0), dtype=BF16)


def _split_bf16(x):
    hi = x.astype(BF16)
    lo = (x - hi.astype(F32)).astype(BF16)
    return jnp.concatenate([hi, lo], axis=1)


def _sb_kernel(q_ref, k_ref, v_ref, g_ref, u_ref, o_ref, acc_ref, run_ref):
    t = SB_TILE
    hw = HEAD_DIM
    n_heads = q_ref.shape[1] // hw
    qi = pl.program_id(2)
    u = u_ref[...]
    row = lax.broadcasted_iota(jnp.int32, (t, t), 0)
    col = lax.broadcasted_iota(jnp.int32, (t, t), 1)
    causal = col < row

    def tile(j, diagonal):
        start = pl.multiple_of(j * t, t)
        logits = []
        split_parts = []
        for h in range(n_heads):
            cols = slice(h * hw, (h + 1) * hw)
            z = lax.dot_general(q_ref[:, cols], k_ref[pl.ds(start, t), cols], NT_DIMS,
                                preferred_element_type=F32)
            neg_z = -z
            l = jnp.minimum(neg_z, 0.0) - jnp.log2(1.0 + jnp.exp2(jnp.minimum(z, neg_z)))
            if diagonal:
                l = jnp.where(causal, l, 0.0)
            logits.append(z)
            split_parts += [_split_bf16(l[:, :hw]), _split_bf16(l[:, hw:])]
        sums = jnp.dot(jnp.concatenate(split_parts, axis=0), u, preferred_element_type=F32)
        for h in range(n_heads):
            cols = slice(h * hw, (h + 1) * hw)
            early = sums[(2 * h) * t:(2 * h + 1) * t]
            late = sums[(2 * h + 1) * t:(2 * h + 2) * t]
            after_late = 0.0 if diagonal else run_ref[:, cols]
            after_early = after_late + late[:, hw:]
            z = logits[h]
            a = jnp.exp2(jnp.concatenate([z[:, :hw] + early[:, :hw] + after_early,
                                          z[:, hw:] + late[:, :hw] + after_late], axis=1))
            if diagonal:
                a = jnp.where(causal, a, 0.0)
            pv = jnp.dot(a.astype(BF16), v_ref[pl.ds(start, t), cols], preferred_element_type=F32)
            acc_ref[:, cols] = pv if diagonal else acc_ref[:, cols] + pv
            run_ref[:, cols] = after_early + early[:, hw:]

    tile(qi, True)

    def more(carry):
        return (carry[0] < qi) & (carry[1] > SB_LOG2_UNDERFLOW)

    def body(carry):
        tile(qi - 1 - carry[0], False)
        return carry[0] + 1, jnp.max(run_ref[...])

    lax.while_loop(more, body, (0, jnp.max(run_ref[...])))
    o_ref[...] = (acc_ref[...] * _silu(g_ref[...].astype(F32))).astype(o_ref.dtype)


def _sb_attention(qkv, gate, n_heads):
    b, s, _ = qkv.shape
    t = SB_TILE
    assert n_heads % SB_HEADS == 0 and s % t == 0
    w = SB_HEADS * HEAD_DIM
    groups = n_heads // SB_HEADS
    return pl.pallas_call(
        _sb_kernel,
        grid=(b, groups, s // t),
        in_specs=[
            pl.BlockSpec((None, t, w), lambda bi, h, qi: (bi, qi, h)),
            pl.BlockSpec((None, s, w), lambda bi, h, qi: (bi, 0, groups + h)),
            pl.BlockSpec((None, s, w), lambda bi, h, qi: (bi, 0, 2 * groups + h)),
            pl.BlockSpec((None, t, w), lambda bi, h, qi: (bi, qi, h)),
            pl.BlockSpec((2 * HEAD_DIM, 2 * HEAD_DIM), lambda bi, h, qi: (0, 0)),
        ],
        out_specs=pl.BlockSpec((None, t, w), lambda bi, h, qi: (bi, qi, h)),
        out_shape=jax.ShapeDtypeStruct((b, s, n_heads * HEAD_DIM), BF16),
        scratch_shapes=[pltpu.VMEM((t, w), F32), pltpu.VMEM((t, w), F32)],
        compiler_params=_compiler_params(("parallel", "parallel", "arbitrary")),
        name="stickbreak_attention",
    )(qkv, qkv, qkv, gate, _suffix_sum_matrix())


def _dil_kernel(q_ref, k_ref, v_ref, g_ref, o_ref, qs_ref, ks_ref, vs_ref, ms_ref, ls_ref, as_ref):
    s_len = q_ref.shape[0]
    blk = DIL_BLOCK
    inner = DIL_INNER_STRIDE
    outer = DILATED_GROUPS[2][1] // inner
    assert DILATED_GROUPS[0][1] == 1 and DILATED_GROUPS[1][1] == inner and outer == inner
    part = s_len // inner
    n_mid = part // blk
    n_wide = part // (outer * blk)
    scale = 1.0 / math.sqrt(HEAD_DIM)
    qi = lax.broadcasted_iota(jnp.int32, (blk, 2 * blk), 0)
    kj = lax.broadcasted_iota(jnp.int32, (blk, 2 * blk), 1)
    band = (kj >= qi) & (kj <= qi + blk)
    band_first = (lax.broadcasted_iota(jnp.int32, (blk, blk), 1)
                  <= lax.broadcasted_iota(jnp.int32, (blk, blk), 0))

    def scores(q, keys, valid):
        sc = lax.dot_general(q.astype(BF16), keys.astype(BF16), NT_DIMS, preferred_element_type=F32)
        return jnp.where(valid, sc, NEG_BIG)

    def weigh(sc, m_new, vals):
        p = jnp.concatenate([jnp.exp(sc[:, i:i + blk] - m_new) for i in range(0, sc.shape[1], blk)], axis=1)
        pv = jnp.dot(p.astype(BF16),
                     jnp.concatenate([vals.astype(BF16), jnp.ones(vals.shape, BF16)], axis=1),
                     preferred_element_type=F32)
        return pv[:, :HEAD_DIM], pv[:, HEAD_DIM:]

    def update(sc, vals, old):
        m_old, l_old, acc_old = old
        m_new = jnp.maximum(m_old, jnp.max(sc, axis=-1, keepdims=True))
        num, den = weigh(sc, m_new, vals)
        alpha = jnp.exp(m_old - m_new)
        return m_new, alpha * l_old + den, alpha * acc_old + num

    def split(i, carry):
        for p in range(inner):
            src = pl.ds(p + inner * blk * i, blk, stride=inner)
            dst = pl.ds(pl.multiple_of(p * part + blk * i, blk), blk)
            qs_ref[dst, :] = q_ref[src, :] * scale
            ks_ref[dst, :] = k_ref[src, :]
            vs_ref[dst, :] = v_ref[src, :]
        return carry

    lax.fori_loop(0, part // blk, split, 0)

    def mid_group(p, carry):
        base = pl.multiple_of(p * part, blk)
        for n in range(n_mid):
            cur = pl.ds(base + n * blk, blk)
            if n == 0:
                sc = scores(qs_ref[cur, :], ks_ref[cur, :], band_first)
                vals = vs_ref[cur, :]
            else:
                both = pl.ds(base + (n - 1) * blk, 2 * blk)
                sc = scores(qs_ref[cur, :], ks_ref[both, :], band)
                vals = vs_ref[both, :]
            m_new = jnp.broadcast_to(jnp.max(sc, axis=-1, keepdims=True), (blk, HEAD_DIM))
            num, den = weigh(sc, m_new, vals)
            ms_ref[cur, :] = m_new
            ls_ref[cur, :] = den
            as_ref[cur, :] = num
        return carry

    lax.fori_loop(0, inner, mid_group, 0)

    def wide_group(p, carry):
        base = p * part
        loaded = []
        for ph in range(outer):
            for n in range(n_wide):
                cur = pl.ds(base + ph + outer * blk * n, blk, stride=outer)
                if n == 0:
                    sc = scores(qs_ref[cur, :], ks_ref[cur, :], band_first)
                    vals = vs_ref[cur, :]
                else:
                    prev = pl.ds(base + ph + outer * blk * (n - 1), blk, stride=outer)
                    sc = scores(qs_ref[cur, :], jnp.concatenate([ks_ref[prev, :], ks_ref[cur, :]], axis=0), band)
                    vals = jnp.concatenate([vs_ref[prev, :], vs_ref[cur, :]], axis=0)
                loaded.append((cur, sc, vals, (ms_ref[cur, :], ls_ref[cur, :], as_ref[cur, :])))
        results = [(cur,) + update(sc, vals, old) for cur, sc, vals, old in loaded]
        for cur, m_new, l_new, acc_new in results:
            ms_ref[cur, :] = m_new
            ls_ref[cur, :] = l_new
            as_ref[cur, :] = acc_new
        return carry

    lax.fori_loop(0, inner, wide_group, 0)

    m_ref, l_ref, acc_ref = qs_ref, ks_ref, vs_ref

    def merge(i, carry):
        for p in range(inner):
            src = pl.ds(pl.multiple_of(p * part + blk * i, blk), blk)
            dst = pl.ds(p + inner * blk * i, blk, stride=inner)
            m_ref[dst, :] = ms_ref[src, :]
            l_ref[dst, :] = ls_ref[src, :]
            acc_ref[dst, :] = as_ref[src, :]
        return carry

    lax.fori_loop(0, part // blk, merge, 0)

    def dense_group(it, carry):
        loaded = []
        for u in range(DIL_UNROLL):
            n = it * DIL_UNROLL + u
            cur = pl.ds(pl.multiple_of(n * blk, blk), blk)
            prev = pl.ds(pl.multiple_of(jnp.maximum(n - 1, 0) * blk, blk), blk)
            keys = jnp.concatenate([k_ref[prev, :], k_ref[cur, :]], axis=0)
            vals = jnp.concatenate([v_ref[prev, :], v_ref[cur, :]], axis=0)
            sc = scores(q_ref[cur, :] * scale, keys, band & ((kj >= blk) | (n > 0)))
            loaded.append((cur, sc, vals, (m_ref[cur, :], l_ref[cur, :], acc_ref[cur, :])))
        for cur, sc, vals, old in loaded:
            _, l_new, acc_new = update(sc, vals, old)
            o_ref[cur, :] = (acc_new / l_new * _silu(g_ref[cur, :].astype(F32))).astype(o_ref.dtype)
        return carry

    lax.fori_loop(0, s_len // (blk * DIL_UNROLL), dense_group, 0)


def _dilated_attention(qk, v, gate, n_heads, gate_col0):
    b, s, _ = v.shape
    seq = lambda off: pl.BlockSpec((None, s, HEAD_DIM), lambda bi, h: (bi, 0, off + h))
    return pl.pallas_call(
        _dil_kernel,
        grid=(b, n_heads),
        in_specs=[seq(0), seq(n_heads), seq(0), seq(gate_col0)],
        out_specs=seq(0),
        out_shape=jax.ShapeDtypeStruct((b, s, n_heads * HEAD_DIM), BF16),
        scratch_shapes=[pltpu.VMEM((s, HEAD_DIM), F32)] * 6,
        compiler_params=_compiler_params(("parallel", "parallel")),
        name="dilated_attention",
    )(qk, qk, v, gate)


def _hgrn_sum_matrix():
    c = HGRN_CHUNK
    t = np.arange(c)[:, None]
    j = np.arange(c)[None, :]
    blocks = [(j <= t), (j > t)]
    for lev in range(HGRN_LEVELS):
        m = 1 << lev
        mid = (t // (2 * m)) * (2 * m) + m - 1
        second = ((t // m) % 2) == 1
        blocks.append(np.where(second, (j > mid) & (j <= t), (j > t) & (j <= mid)))
    mat = np.concatenate(blocks, axis=0).astype(np.float32)
    return jnp.asarray(np.concatenate([mat, mat], axis=1), dtype=BF16)


def _hgrn_kernel(layer, q_ref, f_ref, i_ref, g_ref, lbl_ref, og_ref, sm_ref, o_ref, st_ref):
    c = HGRN_CHUNK
    hw = HEAD_DIM
    n_heads = q_ref.shape[1] // hw

    @pl.when(pl.program_id(2) == 0)
    def _():
        st_ref[...] = jnp.zeros_like(st_ref)

    lbl = lbl_ref[...]
    e = jnp.exp(lbl - jnp.max(lbl, axis=0, keepdims=True))
    share = e / jnp.sum(e, axis=0, keepdims=True)
    lb_all = jnp.sum(share[:layer + 1], axis=0, keepdims=True) - share[0:1]

    row = lax.broadcasted_iota(jnp.int32, (c, c), 0)
    col = lax.broadcasted_iota(jnp.int32, (c, c), 1)

    keys = []
    his = []
    los = []
    totals = []
    for h in range(n_heads):
        cols = slice(h * hw, (h + 1) * hw)
        lb = lb_all[:, cols]
        f = lb + (1.0 - lb) * jax.nn.sigmoid(f_ref[:, cols])
        keys.append(1.0 - f)
        log2f = jnp.log2(f)
        totals.append(jnp.sum(log2f, axis=0, keepdims=True))
        hi = log2f.astype(BF16)
        his.append(hi)
        los.append((log2f - hi.astype(F32)).astype(BF16))
    rhs = jnp.concatenate([jnp.concatenate(his, axis=1), jnp.concatenate(los, axis=1)], axis=0)
    prefix_all = jnp.dot(sm_ref[0:c, :], rhs, preferred_element_type=F32)
    direct_ok = jnp.min(jnp.concatenate(totals, axis=1)) >= -HGRN_DIRECT_LOG2_MAX

    def finish(h, out, state, inp, key_tail, last_decay):
        cols = slice(h * hw, (h + 1) * hw)
        st_ref[h] = state * last_decay + lax.dot_general(inp, key_tail, TN_DIMS, preferred_element_type=F32)
        ms = jnp.mean(out * out, axis=-1, keepdims=True)
        o_ref[:, cols] = (out * lax.rsqrt(ms + RMS_EPS) * og_ref[...]
                          * _silu(g_ref[:, cols].astype(F32))).astype(o_ref.dtype)

    @pl.when(direct_ok)
    def _():
        keep = col <= row
        for h in range(n_heads):
            cols = slice(h * hw, (h + 1) * hw)
            prefix = prefix_all[:, cols]
            grow = jnp.exp2(-prefix)
            q_dec = (q_ref[:, cols].astype(F32) * jnp.exp2(prefix)).astype(BF16)
            key_grow = keys[h] * grow
            inp = i_ref[:, cols]
            state = st_ref[h]
            attn = jnp.where(keep, lax.dot_general(q_dec, key_grow.astype(BF16), NT_DIMS,
                                                   preferred_element_type=F32), 0.0)
            out = (lax.dot_general(q_dec, state.astype(BF16), NT_DIMS, preferred_element_type=F32)
                   + jnp.dot(attn.astype(BF16), inp, preferred_element_type=F32))
            last_decay = jnp.exp2(prefix[c - 1:c, :])
            finish(h, out, state, inp, (key_grow * last_decay).astype(BF16), last_decay)

    @pl.when(jnp.logical_not(direct_ok))
    def _():
        lower = col < row
        split_bit = row ^ col
        level_masks = [lower & ((split_bit >> lev) == 1) for lev in range(HGRN_LEVELS)]
        sums_all = jnp.dot(sm_ref[c:, :], rhs, preferred_element_type=F32)
        for h in range(n_heads):
            cols = slice(h * hw, (h + 1) * hw)
            sums = sums_all[:, cols]
            prefix = prefix_all[:, cols]
            key = keys[h]
            q = q_ref[:, cols].astype(F32)
            inp = i_ref[:, cols]
            state = st_ref[h]
            out = lax.dot_general((q * jnp.exp2(prefix)).astype(BF16), state.astype(BF16), NT_DIMS,
                                  preferred_element_type=F32)
            attn = jnp.where(row == col,
                             lax.dot_general(q.astype(BF16), key.astype(BF16), NT_DIMS,
                                             preferred_element_type=F32), 0.0)
            for lev in range(HGRN_LEVELS):
                w = jnp.exp2(sums[(1 + lev) * c:(2 + lev) * c])
                pair = lax.dot_general((q * w).astype(BF16), (key * w).astype(BF16), NT_DIMS,
                                       preferred_element_type=F32)
                attn = jnp.where(level_masks[lev], pair, attn)
            out = out + jnp.dot(attn.astype(BF16), inp, preferred_element_type=F32)
            finish(h, out, state, inp, (key * jnp.exp2(sums[0:c])).astype(BF16),
                   jnp.exp2(prefix[c - 1:c, :]))


def _hgrn(q, f_pre, inp_gate, lb_logits, o_gain, layer, n_heads):
    b, s, _ = q.shape
    c = HGRN_CHUNK
    assert n_heads % HGRN_HEADS == 0 and s % c == 0
    w = HGRN_HEADS * HEAD_DIM
    groups = n_heads // HGRN_HEADS
    depth = lb_logits.shape[0]
    tok = lambda off: pl.BlockSpec((None, c, w), lambda bi, h, ci: (bi, ci, off + h))
    return pl.pallas_call(
        functools.partial(_hgrn_kernel, layer),
        grid=(b, groups, s // c),
        in_specs=[tok(0), tok(0), tok(0), tok(groups),
                  pl.BlockSpec((depth, w), lambda bi, h, ci: (0, h)),
                  pl.BlockSpec((1, HEAD_DIM), lambda bi, h, ci: (0, 0)),
                  pl.BlockSpec(((2 + HGRN_LEVELS) * c, 2 * c), lambda bi, h, ci: (0, 0))],
        out_specs=tok(0),
        out_shape=jax.ShapeDtypeStruct((b, s, n_heads * HEAD_DIM), BF16),
        scratch_shapes=[pltpu.VMEM((HGRN_HEADS, HEAD_DIM, HEAD_DIM), F32)],
        compiler_params=_compiler_params(("parallel", "parallel", "arbitrary")),
        name="hgrn2",
    )(q, f_pre, inp_gate, inp_gate, lb_logits, o_gain.reshape(1, HEAD_DIM), _hgrn_sum_matrix())


def _even_layer(x, norm_g, w_in, q_gain, k_gain, w_out):
    b, s, d = x.shape
    d_inner = w_out.shape[0]
    d_half = d_inner // 2
    n_heads = d_half // HEAD_DIM
    m = b * s
    x2d = x.reshape(m, d)
    hdn = _rmsnorm(x2d, norm_g)
    c_sb, c_dil_qk, c_dil_v = 3 * d_half, 5 * d_half, 6 * d_half
    q_scale = math.log2(math.e) / math.sqrt(HEAD_DIM)
    sb_scale = jnp.concatenate([jnp.full((1, d_half), q_scale, F32), jnp.ones((1, 2 * d_half), F32)], axis=1)
    qkv_sb = _project(hdn, w_in, 0, c_sb, BF16, col_scale=sb_scale)
    head_gain = jnp.concatenate([jnp.tile(q_gain, n_heads), jnp.tile(k_gain, n_heads)]).reshape(1, 2 * d_half)
    qk_dil = _project(hdn, w_in, c_sb, c_dil_qk - c_sb, F32, head_gain=head_gain)
    v_dil = _project(hdn, w_in, c_dil_qk, c_dil_v - c_dil_qk, F32)
    gate = _project(hdn, w_in, c_dil_v, d_inner, BF16).reshape(b, s, d_inner)
    mix_sb = _sb_attention(qkv_sb.reshape(b, s, c_sb), gate, n_heads)
    mix_dil = _dilated_attention(qk_dil.reshape(b, s, 2 * d_half), v_dil.reshape(b, s, d_half),
                                 gate, n_heads, n_heads)
    y = _out_project(x2d, [(mix_sb.reshape(m, d_half), w_out[:d_half].astype(BF16)),
                           (mix_dil.reshape(m, d_half), w_out[d_half:].astype(BF16))])
    return y.reshape(b, s, d)


def _odd_layer(x, norm_g, w_in, lb_logits, layer, o_gain, w_out):
    b, s, d = x.shape
    d_inner = w_out.shape[0]
    n_heads = d_inner // HEAD_DIM
    m = b * s
    x2d = x.reshape(m, d)
    hdn = _rmsnorm(x2d, norm_g)
    q = _project(hdn, w_in, 0, d_inner, BF16)
    f_pre = _project(hdn, w_in, d_inner, d_inner, F32)
    inp_gate = _project(hdn, w_in, 2 * d_inner, 2 * d_inner, BF16)
    mix = _hgrn(q.reshape(b, s, d_inner), f_pre.reshape(b, s, d_inner),
                inp_gate.reshape(b, s, 2 * d_inner), lb_logits, o_gain, layer, n_heads)
    y = _out_project(x2d, [(mix.reshape(m, d_inner), w_out.astype(BF16))])
    return y.reshape(b, s, d)


def kernel(x, norm_even, w_in_even, q_norm_even, k_norm_even, w_out_even,
           norm_odd, w_in_odd, lb_logits, o_norm_odd, w_out_odd):
    depth = lb_logits.shape[0]
    for layer in range(depth):
        j = layer // 2
        if layer % 2 == 0:
            x = _even_layer(x, norm_even[j], w_in_even[j], q_norm_even[j], k_norm_even[j], w_out_even[j])
        else:
            x = _odd_layer(x, norm_odd[j], w_in_odd[j], lb_logits, layer, o_norm_odd[j], w_out_odd[j])
    return x
```

```python
import functools
import math

import numpy as np
import jax
import jax.numpy as jnp
from jax import lax
from jax.experimental import pallas as pl
from jax.experimental.pallas import tpu as pltpu

HEAD_DIM = 128
RMS_EPS = 1e-6
DILATED_GROUPS = ((128, 1), (512, 4), (2048, 16))
DIL_BLOCK = 128
DIL_UNROLL = 32
DIL_MID_PHASES_PER_ITER = 4
DIL_WIDE_PHASES_PER_ITER = 1
DIL_INNER_STRIDE = 4
SB_TILE = 256
SB_HEADS = 8
SB_LOG2_UNDERFLOW = -150.0
HGRN_CHUNK = 128
HGRN_LEVELS = 7
HGRN_HEADS = 32
HGRN_DIRECT_LOG2_MAX = 100.0
NEG_BIG = -1e30
VMEM_LIMIT_BYTES = 56 * 1024 * 1024

F32 = jnp.float32
BF16 = jnp.bfloat16
NT_DIMS = (((1,), (1,)), ((), ()))
TN_DIMS = (((0,), (0,)), ((), ()))


def _compiler_params(semantics):
    return pltpu.CompilerParams(dimension_semantics=semantics,
                                vmem_limit_bytes=VMEM_LIMIT_BYTES)


def _silu(x):
    return x * jax.nn.sigmoid(x)


def _rmsnorm_kernel(x_ref, g_ref, o_ref):
    x = x_ref[...]
    ms = jnp.mean(x * x, axis=-1, keepdims=True)
    o_ref[...] = (x * lax.rsqrt(ms + RMS_EPS) * g_ref[...]).astype(o_ref.dtype)


def _rmsnorm(x2d, gain, tm=512):
    m, d = x2d.shape
    return pl.pallas_call(
        _rmsnorm_kernel,
        grid=(m // tm,),
        in_specs=[pl.BlockSpec((tm, d), lambda i: (i, 0)),
                  pl.BlockSpec((1, d), lambda i: (0, 0))],
        out_specs=pl.BlockSpec((tm, d), lambda i: (i, 0)),
        out_shape=jax.ShapeDtypeStruct((m, d), BF16),
        compiler_params=_compiler_params(("parallel",)),
        name="rmsnorm",
    )(x2d, gain.reshape(1, d))


def _resident_weight(w_ref, wb_ref):
    @pl.when(pl.program_id(1) == 0)
    def _():
        wb_ref[...] = w_ref[...].astype(BF16)
    return wb_ref[...]


def _proj_kernel(a_ref, w_ref, o_ref, wb_ref):
    w = _resident_weight(w_ref, wb_ref)
    o_ref[...] = jnp.dot(a_ref[...], w, preferred_element_type=F32).astype(o_ref.dtype)


def _proj_scaled_kernel(a_ref, w_ref, s_ref, o_ref, wb_ref):
    w = _resident_weight(w_ref, wb_ref)
    acc = jnp.dot(a_ref[...], w, preferred_element_type=F32)
    o_ref[...] = (acc * s_ref[...]).astype(o_ref.dtype)


def _proj_headnorm_kernel(a_ref, w_ref, g_ref, o_ref, wb_ref):
    w = _resident_weight(w_ref, wb_ref)
    acc = jnp.dot(a_ref[...], w, preferred_element_type=F32)
    for h in range(acc.shape[1] // HEAD_DIM):
        cols = slice(h * HEAD_DIM, (h + 1) * HEAD_DIM)
        blk = acc[:, cols]
        ms = jnp.mean(blk * blk, axis=-1, keepdims=True)
        o_ref[:, cols] = (blk * lax.rsqrt(ms + RMS_EPS) * g_ref[:, cols]).astype(o_ref.dtype)


def _project(a, w, col0, n, out_dtype, head_gain=None, col_scale=None, tm=1024, tn=1024):
    assert head_gain is None or col_scale is None
    assert col0 % tn == 0 and n % tn == 0
    m, k = a.shape
    jt0 = col0 // tn
    in_specs = [pl.BlockSpec((tm, k), lambda j, i: (i, 0)),
                pl.BlockSpec((k, tn), lambda j, i: (0, jt0 + j))]
    args = [a, w]
    kern = _proj_kernel
    if head_gain is not None or col_scale is not None:
        in_specs.append(pl.BlockSpec((1, tn), lambda j, i: (0, j)))
        args.append(head_gain if head_gain is not None else col_scale)
        kern = _proj_headnorm_kernel if head_gain is not None else _proj_scaled_kernel
    return pl.pallas_call(
        kern,
        grid=(n // tn, m // tm),
        in_specs=in_specs,
        out_specs=pl.BlockSpec((tm, tn), lambda j, i: (i, j)),
        out_shape=jax.ShapeDtypeStruct((m, n), out_dtype),
        scratch_shapes=[pltpu.VMEM((k, tn), BF16)],
        compiler_params=_compiler_params(("parallel", "arbitrary")),
        name="project",
    )(*args)


def _out_proj_kernel(n_pairs, *refs):
    x_ref = refs[0]
    o_ref = refs[-1]
    acc = x_ref[...]
    for p in range(n_pairs):
        acc = acc + jnp.dot(refs[1 + 2 * p][...], refs[2 + 2 * p][...], preferred_element_type=F32)
    o_ref[...] = acc


def _out_project(x2d, pairs, tm=512, tn=1024):
    m, n = x2d.shape
    in_specs = [pl.BlockSpec((tm, tn), lambda j, i: (i, j))]
    args = [x2d]
    for a, w in pairs:
        k = a.shape[1]
        in_specs.append(pl.BlockSpec((tm, k), lambda j, i: (i, 0)))
        in_specs.append(pl.BlockSpec((k, tn), lambda j, i: (0, j)))
        args += [a, w]
    return pl.pallas_call(
        functools.partial(_out_proj_kernel, len(pairs)),
        grid=(n // tn, m // tm),
        in_specs=in_specs,
        out_specs=pl.BlockSpec((tm, tn), lambda j, i: (i, j)),
        out_shape=jax.ShapeDtypeStruct((m, n), F32),
        compiler_params=_compiler_params(("parallel", "parallel")),
        name="out_project",
    )(*args)


def _suffix_sum_matrix():
    t = HEAD_DIM
    sp = np.arange(t)[:, None]
    s = np.arange(t)[None, :]
    half = np.concatenate([(sp >= s).astype(np.float32), np.ones((t, t), np.float32)], axis=1)
    return jnp.asarray(np.concatenate([half, half], axis=0), dtype=BF16)


def _split_bf16(x):
    hi = x.astype(BF16)
    lo = (x - hi.astype(F32)).astype(BF16)
    return jnp.concatenate([hi, lo], axis=1)


def _sb_kernel(q_ref, k_ref, v_ref, g_ref, u_ref, o_ref, acc_ref, run_ref):
    t = SB_TILE
    hw = HEAD_DIM
    n_heads = q_ref.shape[1] // hw
    qi = pl.program_id(2)
    u = u_ref[...]
    row = lax.broadcasted_iota(jnp.int32, (t, t), 0)
    col = lax.broadcasted_iota(jnp.int32, (t, t), 1)
    causal = col < row

    def tile(j, diagonal):
        start = pl.multiple_of(j * t, t)
        logits = []
        split_parts = []
        for h in range(n_heads):
            cols = slice(h * hw, (h + 1) * hw)
            z = lax.dot_general(q_ref[:, cols], k_ref[pl.ds(start, t), cols], NT_DIMS,
                                preferred_element_type=F32)
            neg_z = -z
            l = jnp.minimum(neg_z, 0.0) - jnp.log2(1.0 + jnp.exp2(jnp.minimum(z, neg_z)))
            if diagonal:
                l = jnp.where(causal, l, 0.0)
            logits.append(z)
            split_parts += [_split_bf16(l[:, :hw]), _split_bf16(l[:, hw:])]
        sums = jnp.dot(jnp.concatenate(split_parts, axis=0), u, preferred_element_type=F32)
        for h in range(n_heads):
            cols = slice(h * hw, (h + 1) * hw)
            early = sums[(2 * h) * t:(2 * h + 1) * t]
            late = sums[(2 * h + 1) * t:(2 * h + 2) * t]
            after_late = 0.0 if diagonal else run_ref[:, cols]
            after_early = after_late + late[:, hw:]
            z = logits[h]
            a = jnp.exp2(jnp.concatenate([z[:, :hw] + early[:, :hw] + after_early,
                                          z[:, hw:] + late[:, :hw] + after_late], axis=1))
            if diagonal:
                a = jnp.where(causal, a, 0.0)
            pv = jnp.dot(a.astype(BF16), v_ref[pl.ds(start, t), cols], preferred_element_type=F32)
            acc_ref[:, cols] = pv if diagonal else acc_ref[:, cols] + pv
            run_ref[:, cols] = after_early + early[:, hw:]

    tile(qi, True)

    def more(carry):
        return (carry[0] < qi) & (carry[1] > SB_LOG2_UNDERFLOW)

    def body(carry):
        tile(qi - 1 - carry[0], False)
        return carry[0] + 1, jnp.max(run_ref[...])

    lax.while_loop(more, body, (0, jnp.max(run_ref[...])))
    o_ref[...] = (acc_ref[...] * _silu(g_ref[...].astype(F32))).astype(o_ref.dtype)


def _sb_attention(qkv, gate, n_heads):
    b, s, _ = qkv.shape
    t = SB_TILE
    assert n_heads % SB_HEADS == 0 and s % t == 0
    w = SB_HEADS * HEAD_DIM
    groups = n_heads // SB_HEADS
    return pl.pallas_call(
        _sb_kernel,
        grid=(b, groups, s // t),
        in_specs=[
            pl.BlockSpec((None, t, w), lambda bi, h, qi: (bi, qi, h)),
            pl.BlockSpec((None, s, w), lambda bi, h, qi: (bi, 0, groups + h)),
            pl.BlockSpec((None, s, w), lambda bi, h, qi: (bi, 0, 2 * groups + h)),
            pl.BlockSpec((None, t, w), lambda bi, h, qi: (bi, qi, h)),
            pl.BlockSpec((2 * HEAD_DIM, 2 * HEAD_DIM), lambda bi, h, qi: (0, 0)),
        ],
        out_specs=pl.BlockSpec((None, t, w), lambda bi, h, qi: (bi, qi, h)),
        out_shape=jax.ShapeDtypeStruct((b, s, n_heads * HEAD_DIM), BF16),
        scratch_shapes=[pltpu.VMEM((t, w), F32), pltpu.VMEM((t, w), F32)],
        compiler_params=_compiler_params(("parallel", "parallel", "arbitrary")),
        name="stickbreak_attention",
    )(qkv, qkv, qkv, gate, _suffix_sum_matrix())


def _dil_kernel(q_ref, k_ref, v_ref, g_ref, o_ref, qs_ref, ks_ref, vs_ref, ms_ref, ls_ref, as_ref):
    s_len = q_ref.shape[0]
    blk = DIL_BLOCK
    inner = DIL_INNER_STRIDE
    outer = DILATED_GROUPS[2][1] // inner
    assert DILATED_GROUPS[0][1] == 1 and DILATED_GROUPS[1][1] == inner and outer == inner
    part = s_len // inner
    n_mid = part // blk
    n_wide = part // (outer * blk)
    scale = 1.0 / math.sqrt(HEAD_DIM)
    qi = lax.broadcasted_iota(jnp.int32, (blk, 2 * blk), 0)
    kj = lax.broadcasted_iota(jnp.int32, (blk, 2 * blk), 1)
    band = (kj >= qi) & (kj <= qi + blk)
    band_first = (lax.broadcasted_iota(jnp.int32, (blk, blk), 1)
                  <= lax.broadcasted_iota(jnp.int32, (blk, blk), 0))

    def scores(q, keys, valid):
        sc = lax.dot_general(q.astype(BF16), keys.astype(BF16), NT_DIMS, preferred_element_type=F32)
        return jnp.where(valid, sc, NEG_BIG)

    def weigh(sc, m_new, vals):
        p = jnp.concatenate([jnp.exp(sc[:, i:i + blk] - m_new) for i in range(0, sc.shape[1], blk)], axis=1)
        pv = jnp.dot(p.astype(BF16),
                     jnp.concatenate([vals.astype(BF16), jnp.ones(vals.shape, BF16)], axis=1),
                     preferred_element_type=F32)
        return pv[:, :HEAD_DIM], pv[:, HEAD_DIM:]

    def update(sc, vals, old):
        m_old, l_old, acc_old = old
        m_new = jnp.maximum(m_old, jnp.max(sc, axis=-1, keepdims=True))
        num, den = weigh(sc, m_new, vals)
        alpha = jnp.exp(m_old - m_new)
        return m_new, alpha * l_old + den, alpha * acc_old + num

    def split(i, carry):
        for p in range(inner):
            src = pl.ds(p + inner * blk * i, blk, stride=inner)
            dst = pl.ds(pl.multiple_of(p * part + blk * i, blk), blk)
            qs_ref[dst, :] = q_ref[src, :] * scale
            ks_ref[dst, :] = k_ref[src, :]
            vs_ref[dst, :] = v_ref[src, :]
        return carry

    lax.fori_loop(0, part // blk, split, 0)

    def mid_group(it, carry):
        for pp in range(DIL_MID_PHASES_PER_ITER):
            base = pl.multiple_of((it * DIL_MID_PHASES_PER_ITER + pp) * part, blk)
            for n in range(n_mid):
                cur = pl.ds(base + n * blk, blk)
                if n == 0:
                    sc = scores(qs_ref[cur, :], ks_ref[cur, :], band_first)
                    vals = vs_ref[cur, :]
                else:
                    both = pl.ds(base + (n - 1) * blk, 2 * blk)
                    sc = scores(qs_ref[cur, :], ks_ref[both, :], band)
                    vals = vs_ref[both, :]
                m_new = jnp.broadcast_to(jnp.max(sc, axis=-1, keepdims=True), (blk, HEAD_DIM))
                num, den = weigh(sc, m_new, vals)
                ms_ref[cur, :] = m_new
                ls_ref[cur, :] = den
                as_ref[cur, :] = num
        return carry

    lax.fori_loop(0, inner // DIL_MID_PHASES_PER_ITER, mid_group, 0)

    def wide_group(it, carry):
        loaded = []
        for pp in range(DIL_WIDE_PHASES_PER_ITER):
            base = (it * DIL_WIDE_PHASES_PER_ITER + pp) * part
            for ph in range(outer):
                for n in range(n_wide):
                    cur = pl.ds(base + ph + outer * blk * n, blk, stride=outer)
                    if n == 0:
                        sc = scores(qs_ref[cur, :], ks_ref[cur, :], band_first)
                        vals = vs_ref[cur, :]
                    else:
                        prev = pl.ds(base + ph + outer * blk * (n - 1), blk, stride=outer)
                        sc = scores(qs_ref[cur, :], jnp.concatenate([ks_ref[prev, :], ks_ref[cur, :]], axis=0), band)
                        vals = jnp.concatenate([vs_ref[prev, :], vs_ref[cur, :]], axis=0)
                    loaded.append((cur, sc, vals, (ms_ref[cur, :], ls_ref[cur, :], as_ref[cur, :])))
        results = [(cur,) + update(sc, vals, old) for cur, sc, vals, old in loaded]
        for cur, m_new, l_new, acc_new in results:
            ms_ref[cur, :] = m_new
            ls_ref[cur, :] = l_new
            as_ref[cur, :] = acc_new
        return carry

    lax.fori_loop(0, inner // DIL_WIDE_PHASES_PER_ITER, wide_group, 0)

    m_ref, l_ref, acc_ref = qs_ref, ks_ref, vs_ref

    def merge(i, carry):
        for p in range(inner):
            src = pl.ds(pl.multiple_of(p * part + blk * i, blk), blk)
            dst = pl.ds(p + inner * blk * i, blk, stride=inner)
            m_ref[dst, :] = ms_ref[src, :]
            l_ref[dst, :] = ls_ref[src, :]
            acc_ref[dst, :] = as_ref[src, :]
        return carry

    lax.fori_loop(0, part // blk, merge, 0)

    def dense_group(it, carry):
        loaded = []
        for u in range(DIL_UNROLL):
            n = it * DIL_UNROLL + u
            cur = pl.ds(pl.multiple_of(n * blk, blk), blk)
            prev = pl.ds(pl.multiple_of(jnp.maximum(n - 1, 0) * blk, blk), blk)
            keys = jnp.concatenate([k_ref[prev, :], k_ref[cur, :]], axis=0)
            vals = jnp.concatenate([v_ref[prev, :], v_ref[cur, :]], axis=0)
            sc = scores(q_ref[cur, :] * scale, keys, band & ((kj >= blk) | (n > 0)))
            loaded.append((cur, sc, vals, (m_ref[cur, :], l_ref[cur, :], acc_ref[cur, :])))
        for cur, sc, vals, old in loaded:
            _, l_new, acc_new = update(sc, vals, old)
            o_ref[cur, :] = (acc_new / l_new * _silu(g_ref[cur, :].astype(F32))).astype(o_ref.dtype)
        return carry

    lax.fori_loop(0, s_len // (blk * DIL_UNROLL), dense_group, 0)


def _dilated_attention(qk, v, gate, n_heads, gate_col0):
    b, s, _ = v.shape
    seq = lambda off: pl.BlockSpec((None, s, HEAD_DIM), lambda bi, h: (bi, 0, off + h))
    return pl.pallas_call(
        _dil_kernel,
        grid=(b, n_heads),
        in_specs=[seq(0), seq(n_heads), seq(0), seq(gate_col0)],
        out_specs=seq(0),
        out_shape=jax.ShapeDtypeStruct((b, s, n_heads * HEAD_DIM), BF16),
        scratch_shapes=[pltpu.VMEM((s, HEAD_DIM), F32)] * 6,
        compiler_params=_compiler_params(("parallel", "parallel")),
        name="dilated_attention",
    )(qk, qk, v, gate)


def _hgrn_sum_matrix():
    c = HGRN_CHUNK
    t = np.arange(c)[:, None]
    j = np.arange(c)[None, :]
    blocks = [(j <= t), (j > t)]
    for lev in range(HGRN_LEVELS):
        m = 1 << lev
        mid = (t // (2 * m)) * (2 * m) + m - 1
        second = ((t // m) % 2) == 1
        blocks.append(np.where(second, (j > mid) & (j <= t), (j > t) & (j <= mid)))
    mat = np.concatenate(blocks, axis=0).astype(np.float32)
    return jnp.asarray(np.concatenate([mat, mat], axis=1), dtype=BF16)


def _hgrn_kernel(layer, q_ref, f_ref, i_ref, g_ref, lbl_ref, og_ref, sm_ref, o_ref, st_ref):
    c = HGRN_CHUNK
    hw = HEAD_DIM
    n_heads = q_ref.shape[1] // hw

    @pl.when(pl.program_id(2) == 0)
    def _():
        st_ref[...] = jnp.zeros_like(st_ref)

    lbl = lbl_ref[...]
    e = jnp.exp(lbl - jnp.max(lbl, axis=0, keepdims=True))
    share = e / jnp.sum(e, axis=0, keepdims=True)
    lb_all = jnp.sum(share[:layer + 1], axis=0, keepdims=True) - share[0:1]

    row = lax.broadcasted_iota(jnp.int32, (c, c), 0)
    col = lax.broadcasted_iota(jnp.int32, (c, c), 1)

    keys = []
    his = []
    los = []
    totals = []
    for h in range(n_heads):
        cols = slice(h * hw, (h + 1) * hw)
        lb = lb_all[:, cols]
        f = lb + (1.0 - lb) * jax.nn.sigmoid(f_ref[:, cols])
        keys.append(1.0 - f)
        log2f = jnp.log2(f)
        totals.append(jnp.sum(log2f, axis=0, keepdims=True))
        hi = log2f.astype(BF16)
        his.append(hi)
        los.append((log2f - hi.astype(F32)).astype(BF16))
    rhs = jnp.concatenate([jnp.concatenate(his, axis=1), jnp.concatenate(los, axis=1)], axis=0)
    prefix_all = jnp.dot(sm_ref[0:c, :], rhs, preferred_element_type=F32)
    direct_ok = jnp.min(jnp.concatenate(totals, axis=1)) >= -HGRN_DIRECT_LOG2_MAX

    def finish(h, out, state, inp, key_tail, last_decay):
        cols = slice(h * hw, (h + 1) * hw)
        st_ref[h] = state * last_decay + lax.dot_general(inp, key_tail, TN_DIMS, preferred_element_type=F32)
        ms = jnp.mean(out * out, axis=-1, keepdims=True)
        o_ref[:, cols] = (out * lax.rsqrt(ms + RMS_EPS) * og_ref[...]
                          * _silu(g_ref[:, cols].astype(F32))).astype(o_ref.dtype)

    @pl.when(direct_ok)
    def _():
        keep = col <= row
        for h in range(n_heads):
            cols = slice(h * hw, (h + 1) * hw)
            prefix = prefix_all[:, cols]
            grow = jnp.exp2(-prefix)
            q_dec = (q_ref[:, cols].astype(F32) * jnp.exp2(prefix)).astype(BF16)
            key_grow = keys[h] * grow
            inp = i_ref[:, cols]
            state = st_ref[h]
            attn = jnp.where(keep, lax.dot_general(q_dec, key_grow.astype(BF16), NT_DIMS,
                                                   preferred_element_type=F32), 0.0)
            out = (lax.dot_general(q_dec, state.astype(BF16), NT_DIMS, preferred_element_type=F32)
                   + jnp.dot(attn.astype(BF16), inp, preferred_element_type=F32))
            last_decay = jnp.exp2(prefix[c - 1:c, :])
            finish(h, out, state, inp, (key_grow * last_decay).astype(BF16), last_decay)

    @pl.when(jnp.logical_not(direct_ok))
    def _():
        lower = col < row
        split_bit = row ^ col
        level_masks = [lower & ((split_bit >> lev) == 1) for lev in range(HGRN_LEVELS)]
        sums_all = jnp.dot(sm_ref[c:, :], rhs, preferred_element_type=F32)
        for h in range(n_heads):
            cols = slice(h * hw, (h + 1) * hw)
            sums = sums_all[:, cols]
            prefix = prefix_all[:, cols]
            key = keys[h]
            q = q_ref[:, cols].astype(F32)
            inp = i_ref[:, cols]
            state = st_ref[h]
            out = lax.dot_general((q * jnp.exp2(prefix)).astype(BF16), state.astype(BF16), NT_DIMS,
                                  preferred_element_type=F32)
            attn = jnp.where(row == col,
                             lax.dot_general(q.astype(BF16), key.astype(BF16), NT_DIMS,
                                             preferred_element_type=F32), 0.0)
            for lev in range(HGRN_LEVELS):
                w = jnp.exp2(sums[(1 + lev) * c:(2 + lev) * c])
                pair = lax.dot_general((q * w).astype(BF16), (key * w).astype(BF16), NT_DIMS,
                                       preferred_element_type=F32)
                attn = jnp.where(level_masks[lev], pair, attn)
            out = out + jnp.dot(attn.astype(BF16), inp, preferred_element_type=F32)
            finish(h, out, state, inp, (key * jnp.exp2(sums[0:c])).astype(BF16),
                   jnp.exp2(prefix[c - 1:c, :]))


def _hgrn(q, f_pre, inp_gate, lb_logits, o_gain, layer, n_heads):
    b, s, _ = q.shape
    c = HGRN_CHUNK
    assert n_heads % HGRN_HEADS == 0 and s % c == 0
    w = HGRN_HEADS * HEAD_DIM
    groups = n_heads // HGRN_HEADS
    depth = lb_logits.shape[0]
    tok = lambda off: pl.BlockSpec((None, c, w), lambda bi, h, ci: (bi, ci, off + h))
    return pl.pallas_call(
        functools.partial(_hgrn_kernel, layer),
        grid=(b, groups, s // c),
        in_specs=[tok(0), tok(0), tok(0), tok(groups),
                  pl.BlockSpec((depth, w), lambda bi, h, ci: (0, h)),
                  pl.BlockSpec((1, HEAD_DIM), lambda bi, h, ci: (0, 0)),
                  pl.BlockSpec(((2 + HGRN_LEVELS) * c, 2 * c), lambda bi, h, ci: (0, 0))],
        out_specs=tok(0),
        out_shape=jax.ShapeDtypeStruct((b, s, n_heads * HEAD_DIM), BF16),
        scratch_shapes=[pltpu.VMEM((HGRN_HEADS, HEAD_DIM, HEAD_DIM), F32)],
        compiler_params=_compiler_params(("parallel", "parallel", "arbitrary")),
        name="hgrn2",
    )(q, f_pre, inp_gate, inp_gate, lb_logits, o_gain.reshape(1, HEAD_DIM), _hgrn_sum_matrix())


def _even_layer(x, norm_g, w_in, q_gain, k_gain, w_out):
    b, s, d = x.shape
    d_inner = w_out.shape[0]
    d_half = d_inner // 2
    n_heads = d_half // HEAD_DIM
    m = b * s
    x2d = x.reshape(m, d)
    hdn = _rmsnorm(x2d, norm_g)
    c_sb, c_dil_qk, c_dil_v = 3 * d_half, 5 * d_half, 6 * d_half
    q_scale = math.log2(math.e) / math.sqrt(HEAD_DIM)
    sb_scale = jnp.concatenate([jnp.full((1, d_half), q_scale, F32), jnp.ones((1, 2 * d_half), F32)], axis=1)
    qkv_sb = _project(hdn, w_in, 0, c_sb, BF16, col_scale=sb_scale)
    head_gain = jnp.concatenate([jnp.tile(q_gain, n_heads), jnp.tile(k_gain, n_heads)]).reshape(1, 2 * d_half)
    qk_dil = _project(hdn, w_in, c_sb, c_dil_qk - c_sb, F32, head_gain=head_gain)
    v_dil = _project(hdn, w_in, c_dil_qk, c_dil_v - c_dil_qk, F32)
    gate = _project(hdn, w_in, c_dil_v, d_inner, BF16).reshape(b, s, d_inner)
    mix_sb = _sb_attention(qkv_sb.reshape(b, s, c_sb), gate, n_heads)
    mix_dil = _dilated_attention(qk_dil.reshape(b, s, 2 * d_half), v_dil.reshape(b, s, d_half),
                                 gate, n_heads, n_heads)
    y = _out_project(x2d, [(mix_sb.reshape(m, d_half), w_out[:d_half].astype(BF16)),
                           (mix_dil.reshape(m, d_half), w_out[d_half:].astype(BF16))])
    return y.reshape(b, s, d)


def _odd_layer(x, norm_g, w_in, lb_logits, layer, o_gain, w_out):
    b, s, d = x.shape
    d_inner = w_out.shape[0]
    n_heads = d_inner // HEAD_DIM
    m = b * s
    x2d = x.reshape(m, d)
    hdn = _rmsnorm(x2d, norm_g)
    q = _project(hdn, w_in, 0, d_inner, BF16)
    f_pre = _project(hdn, w_in, d_inner, d_inner, F32)
    inp_gate = _project(hdn, w_in, 2 * d_inner, 2 * d_inner, BF16)
    mix = _hgrn(q.reshape(b, s, d_inner), f_pre.reshape(b, s, d_inner),
                inp_gate.reshape(b, s, 2 * d_inner), lb_logits, o_gain, layer, n_heads)
    y = _out_project(x2d, [(mix.reshape(m, d_inner), w_out.astype(BF16))])
    return y.reshape(b, s, d)


def kernel(x, norm_even, w_in_even, q_norm_even, k_norm_even, w_out_even,
           norm_odd, w_in_odd, lb_logits, o_norm_odd, w_out_odd):
    depth = lb_logits.shape[0]
    for layer in range(depth):
        j = layer // 2
        if layer % 2 == 0:
            x = _even_layer(x, norm_even[j], w_in_even[j], q_norm_even[j], k_norm_even[j], w_out_even[j])
        else:
            x = _odd_layer(x, norm_odd[j], w_in_odd[j], lb_logits, layer, o_norm_odd[j], w_out_odd[j])
    return x
```

```python
import functools
import math

import numpy as np
import jax
import jax.numpy as jnp
from jax import lax
from jax.experimental import pallas as pl
from jax.experimental.pallas import tpu as pltpu

HEAD_DIM = 128
RMS_EPS = 1e-6
DILATED_GROUPS = ((128, 1), (512, 4), (2048, 16))
DIL_BLOCK = 128
DIL_UNROLL = 32
DIL_MID_PHASES_PER_ITER = 4
DIL_WIDE_PHASES_PER_ITER = 1
DIL_INNER_STRIDE = 4
DIL_SCORE_BOUND = 60.0
SB_TILE = 256
SB_HEADS = 8
SB_LOG2_UNDERFLOW = -150.0
HGRN_CHUNK = 128
HGRN_LEVELS = 7
HGRN_HEADS = 32
HGRN_DIRECT_LOG2_MAX = 100.0
NEG_BIG = -1e30
VMEM_LIMIT_BYTES = 56 * 1024 * 1024

F32 = jnp.float32
BF16 = jnp.bfloat16
NT_DIMS = (((1,), (1,)), ((), ()))
TN_DIMS = (((0,), (0,)), ((), ()))


def _compiler_params(semantics):
    return pltpu.CompilerParams(dimension_semantics=semantics,
                                vmem_limit_bytes=VMEM_LIMIT_BYTES)


def _silu(x):
    return x * jax.nn.sigmoid(x)


def _rmsnorm_kernel(x_ref, g_ref, o_ref):
    x = x_ref[...]
    ms = jnp.mean(x * x, axis=-1, keepdims=True)
    o_ref[...] = (x * lax.rsqrt(ms + RMS_EPS) * g_ref[...]).astype(o_ref.dtype)


def _rmsnorm(x2d, gain, tm=512):
    m, d = x2d.shape
    return pl.pallas_call(
        _rmsnorm_kernel,
        grid=(m // tm,),
        in_specs=[pl.BlockSpec((tm, d), lambda i: (i, 0)),
                  pl.BlockSpec((1, d), lambda i: (0, 0))],
        out_specs=pl.BlockSpec((tm, d), lambda i: (i, 0)),
        out_shape=jax.ShapeDtypeStruct((m, d), BF16),
        compiler_params=_compiler_params(("parallel",)),
        name="rmsnorm",
    )(x2d, gain.reshape(1, d))


def _resident_weight(w_ref, wb_ref):
    @pl.when(pl.program_id(1) == 0)
    def _():
        wb_ref[...] = w_ref[...].astype(BF16)
    return wb_ref[...]


def _proj_kernel(a_ref, w_ref, o_ref, wb_ref):
    w = _resident_weight(w_ref, wb_ref)
    o_ref[...] = jnp.dot(a_ref[...], w, preferred_element_type=F32).astype(o_ref.dtype)


def _proj_scaled_kernel(a_ref, w_ref, s_ref, o_ref, wb_ref):
    w = _resident_weight(w_ref, wb_ref)
    acc = jnp.dot(a_ref[...], w, preferred_element_type=F32)
    o_ref[...] = (acc * s_ref[...]).astype(o_ref.dtype)


def _proj_headnorm_kernel(a_ref, w_ref, g_ref, o_ref, wb_ref):
    w = _resident_weight(w_ref, wb_ref)
    acc = jnp.dot(a_ref[...], w, preferred_element_type=F32)
    for h in range(acc.shape[1] // HEAD_DIM):
        cols = slice(h * HEAD_DIM, (h + 1) * HEAD_DIM)
        blk = acc[:, cols]
        ms = jnp.mean(blk * blk, axis=-1, keepdims=True)
        o_ref[:, cols] = (blk * lax.rsqrt(ms + RMS_EPS) * g_ref[:, cols]).astype(o_ref.dtype)


def _project(a, w, col0, n, out_dtype, head_gain=None, col_scale=None, tm=1024, tn=1024):
    assert head_gain is None or col_scale is None
    assert col0 % tn == 0 and n % tn == 0
    m, k = a.shape
    jt0 = col0 // tn
    in_specs = [pl.BlockSpec((tm, k), lambda j, i: (i, 0)),
                pl.BlockSpec((k, tn), lambda j, i: (0, jt0 + j))]
    args = [a, w]
    kern = _proj_kernel
    if head_gain is not None or col_scale is not None:
        in_specs.append(pl.BlockSpec((1, tn), lambda j, i: (0, j)))
        args.append(head_gain if head_gain is not None else col_scale)
        kern = _proj_headnorm_kernel if head_gain is not None else _proj_scaled_kernel
    return pl.pallas_call(
        kern,
        grid=(n // tn, m // tm),
        in_specs=in_specs,
        out_specs=pl.BlockSpec((tm, tn), lambda j, i: (i, j)),
        out_shape=jax.ShapeDtypeStruct((m, n), out_dtype),
        scratch_shapes=[pltpu.VMEM((k, tn), BF16)],
        compiler_params=_compiler_params(("parallel", "arbitrary")),
        name="project",
    )(*args)


def _out_proj_kernel(n_pairs, *refs):
    x_ref = refs[0]
    o_ref = refs[-1]
    acc = x_ref[...]
    for p in range(n_pairs):
        acc = acc + jnp.dot(refs[1 + 2 * p][...], refs[2 + 2 * p][...], preferred_element_type=F32)
    o_ref[...] = acc


def _out_project(x2d, pairs, tm=512, tn=1024):
    m, n = x2d.shape
    in_specs = [pl.BlockSpec((tm, tn), lambda j, i: (i, j))]
    args = [x2d]
    for a, w in pairs:
        k = a.shape[1]
        in_specs.append(pl.BlockSpec((tm, k), lambda j, i: (i, 0)))
        in_specs.append(pl.BlockSpec((k, tn), lambda j, i: (0, j)))
        args += [a, w]
    return pl.pallas_call(
        functools.partial(_out_proj_kernel, len(pairs)),
        grid=(n // tn, m // tm),
        in_specs=in_specs,
        out_specs=pl.BlockSpec((tm, tn), lambda j, i: (i, j)),
        out_shape=jax.ShapeDtypeStruct((m, n), F32),
        compiler_params=_compiler_params(("parallel", "parallel")),
        name="out_project",
    )(*args)


def _suffix_sum_matrix():
    t = HEAD_DIM
    sp = np.arange(t)[:, None]
    s = np.arange(t)[None, :]
    half = np.concatenate([(sp >= s).astype(np.float32), np.ones((t, t), np.float32)], axis=1)
    return jnp.asarray(np.concatenate([half, half], axis=0), dtype=BF16)


def _split_bf16(x):
    hi = x.astype(BF16)
    lo = (x - hi.astype(F32)).astype(BF16)
    return jnp.concatenate([hi, lo], axis=1)


def _sb_kernel(q_ref, k_ref, v_ref, g_ref, u_ref, o_ref, acc_ref, run_ref):
    t = SB_TILE
    hw = HEAD_DIM
    n_heads = q_ref.shape[1] // hw
    qi = pl.program_id(2)
    u = u_ref[...]
    row = lax.broadcasted_iota(jnp.int32, (t, t), 0)
    col = lax.broadcasted_iota(jnp.int32, (t, t), 1)
    causal = col < row

    def tile(j, diagonal):
        start = pl.multiple_of(j * t, t)
        logits = []
        split_parts = []
        for h in range(n_heads):
            cols = slice(h * hw, (h + 1) * hw)
            z = lax.dot_general(q_ref[:, cols], k_ref[pl.ds(start, t), cols], NT_DIMS,
                                preferred_element_type=F32)
            neg_z = -z
            l = jnp.minimum(neg_z, 0.0) - jnp.log2(1.0 + jnp.exp2(jnp.minimum(z, neg_z)))
            if diagonal:
                l = jnp.where(causal, l, 0.0)
            logits.append(z)
            split_parts += [_split_bf16(l[:, :hw]), _split_bf16(l[:, hw:])]
        sums = jnp.dot(jnp.concatenate(split_parts, axis=0), u, preferred_element_type=F32)
        for h in range(n_heads):
            cols = slice(h * hw, (h + 1) * hw)
            early = sums[(2 * h) * t:(2 * h + 1) * t]
            late = sums[(2 * h + 1) * t:(2 * h + 2) * t]
            after_late = 0.0 if diagonal else run_ref[:, cols]
            after_early = after_late + late[:, hw:]
            z = logits[h]
            a = jnp.exp2(jnp.concatenate([z[:, :hw] + early[:, :hw] + after_early,
                                          z[:, hw:] + late[:, :hw] + after_late], axis=1))
            if diagonal:
                a = jnp.where(causal, a, 0.0)
            pv = jnp.dot(a.astype(BF16), v_ref[pl.ds(start, t), cols], preferred_element_type=F32)
            acc_ref[:, cols] = pv if diagonal else acc_ref[:, cols] + pv
            run_ref[:, cols] = after_early + early[:, hw:]

    tile(qi, True)

    def more(carry):
        return (carry[0] < qi) & (carry[1] > SB_LOG2_UNDERFLOW)

    def body(carry):
        tile(qi - 1 - carry[0], False)
        return carry[0] + 1, jnp.max(run_ref[...])

    lax.while_loop(more, body, (0, jnp.max(run_ref[...])))
    o_ref[...] = (acc_ref[...] * _silu(g_ref[...].astype(F32))).astype(o_ref.dtype)


def _sb_attention(qkv, gate, n_heads):
    b, s, _ = qkv.shape
    t = SB_TILE
    assert n_heads % SB_HEADS == 0 and s % t == 0
    w = SB_HEADS * HEAD_DIM
    groups = n_heads // SB_HEADS
    return pl.pallas_call(
        _sb_kernel,
        grid=(b, groups, s // t),
        in_specs=[
            pl.BlockSpec((None, t, w), lambda bi, h, qi: (bi, qi, h)),
            pl.BlockSpec((None, s, w), lambda bi, h, qi: (bi, 0, groups + h)),
            pl.BlockSpec((None, s, w), lambda bi, h, qi: (bi, 0, 2 * groups + h)),
            pl.BlockSpec((None, t, w), lambda bi, h, qi: (bi, qi, h)),
            pl.BlockSpec((2 * HEAD_DIM, 2 * HEAD_DIM), lambda bi, h, qi: (0, 0)),
        ],
        out_specs=pl.BlockSpec((None, t, w), lambda bi, h, qi: (bi, qi, h)),
        out_shape=jax.ShapeDtypeStruct((b, s, n_heads * HEAD_DIM), BF16),
        scratch_shapes=[pltpu.VMEM((t, w), F32), pltpu.VMEM((t, w), F32)],
        compiler_params=_compiler_params(("parallel", "parallel", "arbitrary")),
        name="stickbreak_attention",
    )(qkv, qkv, qkv, gate, _suffix_sum_matrix())


def _dil_kernel(bounded_ref, q_ref, k_ref, v_ref, g_ref, o_ref,
                qs_ref, ks_ref, vs_ref, ms_ref, ls_ref, as_ref):
    s_len = q_ref.shape[0]
    blk = DIL_BLOCK
    inner = DIL_INNER_STRIDE
    outer = DILATED_GROUPS[2][1] // inner
    assert DILATED_GROUPS[0][1] == 1 and DILATED_GROUPS[1][1] == inner and outer == inner
    part = s_len // inner
    n_mid = part // blk
    n_wide = part // (outer * blk)
    scale = 1.0 / math.sqrt(HEAD_DIM)
    qi = lax.broadcasted_iota(jnp.int32, (blk, 2 * blk), 0)
    kj = lax.broadcasted_iota(jnp.int32, (blk, 2 * blk), 1)
    band = (kj >= qi) & (kj <= qi + blk)
    band_first = (lax.broadcasted_iota(jnp.int32, (blk, blk), 1)
                  <= lax.broadcasted_iota(jnp.int32, (blk, blk), 0))

    def scores(q, keys, valid):
        sc = lax.dot_general(q.astype(BF16), keys.astype(BF16), NT_DIMS, preferred_element_type=F32)
        return jnp.where(valid, sc, NEG_BIG)

    def weigh(sc, m_new, vals):
        halves = [sc[:, i:i + blk] for i in range(0, sc.shape[1], blk)]
        p = jnp.concatenate([jnp.exp(h if m_new is None else h - m_new) for h in halves], axis=1)
        pv = jnp.dot(p.astype(BF16),
                     jnp.concatenate([vals.astype(BF16), jnp.ones(vals.shape, BF16)], axis=1),
                     preferred_element_type=F32)
        return pv[:, :HEAD_DIM], pv[:, HEAD_DIM:]

    def update(sc, vals, old):
        m_old, l_old, acc_old = old
        if m_old is None:
            num, den = weigh(sc, None, vals)
            return None, l_old + den, acc_old + num
        m_new = jnp.maximum(m_old, jnp.max(sc, axis=-1, keepdims=True))
        num, den = weigh(sc, m_new, vals)
        alpha = jnp.exp(m_old - m_new)
        return m_new, alpha * l_old + den, alpha * acc_old + num

    def split(i, carry):
        for p in range(inner):
            src = pl.ds(p + inner * blk * i, blk, stride=inner)
            dst = pl.ds(pl.multiple_of(p * part + blk * i, blk), blk)
            qs_ref[dst, :] = q_ref[src, :] * scale
            ks_ref[dst, :] = k_ref[src, :]
            vs_ref[dst, :] = v_ref[src, :]
        return carry

    lax.fori_loop(0, part // blk, split, 0)

    def groups(track_max):
        def mid_group(it, carry):
            for pp in range(DIL_MID_PHASES_PER_ITER):
                base = pl.multiple_of((it * DIL_MID_PHASES_PER_ITER + pp) * part, blk)
                for n in range(n_mid):
                    cur = pl.ds(base + n * blk, blk)
                    if n == 0:
                        sc = scores(qs_ref[cur, :], ks_ref[cur, :], band_first)
                        vals = vs_ref[cur, :]
                    else:
                        both = pl.ds(base + (n - 1) * blk, 2 * blk)
                        sc = scores(qs_ref[cur, :], ks_ref[both, :], band)
                        vals = vs_ref[both, :]
                    m_new = None
                    if track_max:
                        m_new = jnp.broadcast_to(jnp.max(sc, axis=-1, keepdims=True), (blk, HEAD_DIM))
                        ms_ref[cur, :] = m_new
                    num, den = weigh(sc, m_new, vals)
                    ls_ref[cur, :] = den
                    as_ref[cur, :] = num
            return carry

        lax.fori_loop(0, inner // DIL_MID_PHASES_PER_ITER, mid_group, 0)

        def wide_group(it, carry):
            loaded = []
            for pp in range(DIL_WIDE_PHASES_PER_ITER):
                base = (it * DIL_WIDE_PHASES_PER_ITER + pp) * part
                for ph in range(outer):
                    for n in range(n_wide):
                        cur = pl.ds(base + ph + outer * blk * n, blk, stride=outer)
                        if n == 0:
                            sc = scores(qs_ref[cur, :], ks_ref[cur, :], band_first)
                            vals = vs_ref[cur, :]
                        else:
                            prev = pl.ds(base + ph + outer * blk * (n - 1), blk, stride=outer)
                            sc = scores(qs_ref[cur, :],
                                        jnp.concatenate([ks_ref[prev, :], ks_ref[cur, :]], axis=0), band)
                            vals = jnp.concatenate([vs_ref[prev, :], vs_ref[cur, :]], axis=0)
                        m_old = ms_ref[cur, :] if track_max else None
                        loaded.append((cur, sc, vals, (m_old, ls_ref[cur, :], as_ref[cur, :])))
            results = [(cur,) + update(sc, vals, old) for cur, sc, vals, old in loaded]
            for cur, m_new, l_new, acc_new in results:
                if track_max:
                    ms_ref[cur, :] = m_new
                ls_ref[cur, :] = l_new
                as_ref[cur, :] = acc_new
            return carry

        lax.fori_loop(0, inner // DIL_WIDE_PHASES_PER_ITER, wide_group, 0)

        m_ref, l_ref, acc_ref = qs_ref, ks_ref, vs_ref

        def merge(i, carry):
            for p in range(inner):
                src = pl.ds(pl.multiple_of(p * part + blk * i, blk), blk)
                dst = pl.ds(p + inner * blk * i, blk, stride=inner)
                if track_max:
                    m_ref[dst, :] = ms_ref[src, :]
                l_ref[dst, :] = ls_ref[src, :]
                acc_ref[dst, :] = as_ref[src, :]
            return carry

        lax.fori_loop(0, part // blk, merge, 0)

        def dense_group(it, carry):
            loaded = []
            for u in range(DIL_UNROLL):
                n = it * DIL_UNROLL + u
                cur = pl.ds(pl.multiple_of(n * blk, blk), blk)
                prev = pl.ds(pl.multiple_of(jnp.maximum(n - 1, 0) * blk, blk), blk)
                keys = jnp.concatenate([k_ref[prev, :], k_ref[cur, :]], axis=0)
                vals = jnp.concatenate([v_ref[prev, :], v_ref[cur, :]], axis=0)
                sc = scores(q_ref[cur, :] * scale, keys, band & ((kj >= blk) | (n > 0)))
                m_old = m_ref[cur, :] if track_max else None
                loaded.append((cur, sc, vals, (m_old, l_ref[cur, :], acc_ref[cur, :])))
            for cur, sc, vals, old in loaded:
                _, l_new, acc_new = update(sc, vals, old)
                o_ref[cur, :] = (acc_new / l_new * _silu(g_ref[cur, :].astype(F32))).astype(o_ref.dtype)
            return carry

        lax.fori_loop(0, s_len // (blk * DIL_UNROLL), dense_group, 0)

    bounded = bounded_ref[0] != 0
    pl.when(bounded)(lambda: groups(False))
    pl.when(jnp.logical_not(bounded))(lambda: groups(True))


def _dilated_attention(qk, v, gate, n_heads, gate_col0, q_gain, k_gain):
    b, s, _ = v.shape
    score_bound = math.sqrt(HEAD_DIM) * jnp.max(jnp.abs(q_gain)) * jnp.max(jnp.abs(k_gain))
    bounded = (score_bound <= DIL_SCORE_BOUND).astype(jnp.int32).reshape(1)
    seq = lambda off: pl.BlockSpec((None, s, HEAD_DIM), lambda bi, h: (bi, 0, off + h))
    return pl.pallas_call(
        _dil_kernel,
        grid=(b, n_heads),
        in_specs=[pl.BlockSpec(memory_space=pltpu.SMEM), seq(0), seq(n_heads), seq(0), seq(gate_col0)],
        out_specs=seq(0),
        out_shape=jax.ShapeDtypeStruct((b, s, n_heads * HEAD_DIM), BF16),
        scratch_shapes=[pltpu.VMEM((s, HEAD_DIM), F32)] * 6,
        compiler_params=_compiler_params(("parallel", "parallel")),
        name="dilated_attention",
    )(bounded, qk, qk, v, gate)


def _hgrn_sum_matrix():
    c = HGRN_CHUNK
    t = np.arange(c)[:, None]
    j = np.arange(c)[None, :]
    blocks = [(j <= t), (j > t)]
    for lev in range(HGRN_LEVELS):
        m = 1 << lev
        mid = (t // (2 * m)) * (2 * m) + m - 1
        second = ((t // m) % 2) == 1
        blocks.append(np.where(second, (j > mid) & (j <= t), (j > t) & (j <= mid)))
    mat = np.concatenate(blocks, axis=0).astype(np.float32)
    return jnp.asarray(np.concatenate([mat, mat], axis=1), dtype=BF16)


def _hgrn_kernel(layer, q_ref, f_ref, i_ref, g_ref, lbl_ref, og_ref, sm_ref, o_ref, st_ref):
    c = HGRN_CHUNK
    hw = HEAD_DIM
    n_heads = q_ref.shape[1] // hw

    @pl.when(pl.program_id(2) == 0)
    def _():
        st_ref[...] = jnp.zeros_like(st_ref)

    lbl = lbl_ref[...]
    e = jnp.exp(lbl - jnp.max(lbl, axis=0, keepdims=True))
    share = e / jnp.sum(e, axis=0, keepdims=True)
    lb_all = jnp.sum(share[:layer + 1], axis=0, keepdims=True) - share[0:1]

    row = lax.broadcasted_iota(jnp.int32, (c, c), 0)
    col = lax.broadcasted_iota(jnp.int32, (c, c), 1)

    keys = []
    his = []
    los = []
    totals = []
    for h in range(n_heads):
        cols = slice(h * hw, (h + 1) * hw)
        lb = lb_all[:, cols]
        f = lb + (1.0 - lb) * jax.nn.sigmoid(f_ref[:, cols])
        keys.append(1.0 - f)
        log2f = jnp.log2(f)
        totals.append(jnp.sum(log2f, axis=0, keepdims=True))
        hi = log2f.astype(BF16)
        his.append(hi)
        los.append((log2f - hi.astype(F32)).astype(BF16))
    rhs = jnp.concatenate([jnp.concatenate(his, axis=1), jnp.concatenate(los, axis=1)], axis=0)
    prefix_all = jnp.dot(sm_ref[0:c, :], rhs, preferred_element_type=F32)
    direct_ok = jnp.min(jnp.concatenate(totals, axis=1)) >= -HGRN_DIRECT_LOG2_MAX

    def finish(h, out, state, inp, key_tail, last_decay):
        cols = slice(h * hw, (h + 1) * hw)
        st_ref[h] = state * last_decay + lax.dot_general(inp, key_tail, TN_DIMS, preferred_element_type=F32)
        ms = jnp.mean(out * out, axis=-1, keepdims=True)
        o_ref[:, cols] = (out * lax.rsqrt(ms + RMS_EPS) * og_ref[...]
                          * _silu(g_ref[:, cols].astype(F32))).astype(o_ref.dtype)

    @pl.when(direct_ok)
    def _():
        keep = col <= row
        for h in range(n_heads):
            cols = slice(h * hw, (h + 1) * hw)
            prefix = prefix_all[:, cols]
            grow = jnp.exp2(-prefix)
            q_dec = (q_ref[:, cols].astype(F32) * jnp.exp2(prefix)).astype(BF16)
            key_grow = keys[h] * grow
            inp = i_ref[:, cols]
            state = st_ref[h]
            attn = jnp.where(keep, lax.dot_general(q_dec, key_grow.astype(BF16), NT_DIMS,
                                                   preferred_element_type=F32), 0.0)
            out = (lax.dot_general(q_dec, state.astype(BF16), NT_DIMS, preferred_element_type=F32)
                   + jnp.dot(attn.astype(BF16), inp, preferred_element_type=F32))
            last_decay = jnp.exp2(prefix[c - 1:c, :])
            finish(h, out, state, inp, (key_grow * last_decay).astype(BF16), last_decay)

    @pl.when(jnp.logical_not(direct_ok))
    def _():
        lower = col < row
        split_bit = row ^ col
        level_masks = [lower & ((split_bit >> lev) == 1) for lev in range(HGRN_LEVELS)]
        sums_all = jnp.dot(sm_ref[c:, :], rhs, preferred_element_type=F32)
        for h in range(n_heads):
            cols = slice(h * hw, (h + 1) * hw)
            sums = sums_all[:, cols]
            prefix = prefix_all[:, cols]
            key = keys[h]
            q = q_ref[:, cols].astype(F32)
            inp = i_ref[:, cols]
            state = st_ref[h]
            out = lax.dot_general((q * jnp.exp2(prefix)).astype(BF16), state.astype(BF16), NT_DIMS,
                                  preferred_element_type=F32)
            attn = jnp.where(row == col,
                             lax.dot_general(q.astype(BF16), key.astype(BF16), NT_DIMS,
                                             preferred_element_type=F32), 0.0)
            for lev in range(HGRN_LEVELS):
                w = jnp.exp2(sums[(1 + lev) * c:(2 + lev) * c])
                pair = lax.dot_general((q * w).astype(BF16), (key * w).astype(BF16), NT_DIMS,
                                       preferred_element_type=F32)
                attn = jnp.where(level_masks[lev], pair, attn)
            out = out + jnp.dot(attn.astype(BF16), inp, preferred_element_type=F32)
            finish(h, out, state, inp, (key * jnp.exp2(sums[0:c])).astype(BF16),
                   jnp.exp2(prefix[c - 1:c, :]))


def _hgrn(q, f_pre, inp_gate, lb_logits, o_gain, layer, n_heads):
    b, s, _ = q.shape
    c = HGRN_CHUNK
    assert n_heads % HGRN_HEADS == 0 and s % c == 0
    w = HGRN_HEADS * HEAD_DIM
    groups = n_heads // HGRN_HEADS
    depth = lb_logits.shape[0]
    tok = lambda off: pl.BlockSpec((None, c, w), lambda bi, h, ci: (bi, ci, off + h))
    return pl.pallas_call(
        functools.partial(_hgrn_kernel, layer),
        grid=(b, groups, s // c),
        in_specs=[tok(0), tok(0), tok(0), tok(groups),
                  pl.BlockSpec((depth, w), lambda bi, h, ci: (0, h)),
                  pl.BlockSpec((1, HEAD_DIM), lambda bi, h, ci: (0, 0)),
                  pl.BlockSpec(((2 + HGRN_LEVELS) * c, 2 * c), lambda bi, h, ci: (0, 0))],
        out_specs=tok(0),
        out_shape=jax.ShapeDtypeStruct((b, s, n_heads * HEAD_DIM), BF16),
        scratch_shapes=[pltpu.VMEM((HGRN_HEADS, HEAD_DIM, HEAD_DIM), F32)],
        compiler_params=_compiler_params(("parallel", "parallel", "arbitrary")),
        name="hgrn2",
    )(q, f_pre, inp_gate, inp_gate, lb_logits, o_gain.reshape(1, HEAD_DIM), _hgrn_sum_matrix())


def _even_layer(x, norm_g, w_in, q_gain, k_gain, w_out):
    b, s, d = x.shape
    d_inner = w_out.shape[0]
    d_half = d_inner // 2
    n_heads = d_half // HEAD_DIM
    m = b * s
    x2d = x.reshape(m, d)
    hdn = _rmsnorm(x2d, norm_g)
    c_sb, c_dil_qk, c_dil_v = 3 * d_half, 5 * d_half, 6 * d_half
    q_scale = math.log2(math.e) / math.sqrt(HEAD_DIM)
    sb_scale = jnp.concatenate([jnp.full((1, d_half), q_scale, F32), jnp.ones((1, 2 * d_half), F32)], axis=1)
    qkv_sb = _project(hdn, w_in, 0, c_sb, BF16, col_scale=sb_scale)
    head_gain = jnp.concatenate([jnp.tile(q_gain, n_heads), jnp.tile(k_gain, n_heads)]).reshape(1, 2 * d_half)
    qk_dil = _project(hdn, w_in, c_sb, c_dil_qk - c_sb, F32, head_gain=head_gain)
    v_dil = _project(hdn, w_in, c_dil_qk, c_dil_v - c_dil_qk, F32)
    gate = _project(hdn, w_in, c_dil_v, d_inner, BF16).reshape(b, s, d_inner)
    mix_sb = _sb_attention(qkv_sb.reshape(b, s, c_sb), gate, n_heads)
    mix_dil = _dilated_attention(qk_dil.reshape(b, s, 2 * d_half), v_dil.reshape(b, s, d_half),
                                 gate, n_heads, n_heads, q_gain, k_gain)
    y = _out_project(x2d, [(mix_sb.reshape(m, d_half), w_out[:d_half].astype(BF16)),
                           (mix_dil.reshape(m, d_half), w_out[d_half:].astype(BF16))])
    return y.reshape(b, s, d)


def _odd_layer(x, norm_g, w_in, lb_logits, layer, o_gain, w_out):
    b, s, d = x.shape
    d_inner = w_out.shape[0]
    n_heads = d_inner // HEAD_DIM
    m = b * s
    x2d = x.reshape(m, d)
    hdn = _rmsnorm(x2d, norm_g)
    q = _project(hdn, w_in, 0, d_inner, BF16)
    f_pre = _project(hdn, w_in, d_inner, d_inner, F32)
    inp_gate = _project(hdn, w_in, 2 * d_inner, 2 * d_inner, BF16)
    mix = _hgrn(q.reshape(b, s, d_inner), f_pre.reshape(b, s, d_inner),
                inp_gate.reshape(b, s, 2 * d_inner), lb_logits, o_gain, layer, n_heads)
    y = _out_project(x2d, [(mix.reshape(m, d_inner), w_out.astype(BF16))])
    return y.reshape(b, s, d)


def kernel(x, norm_even, w_in_even, q_norm_even, k_norm_even, w_out_even,
           norm_odd, w_in_odd, lb_logits, o_norm_odd, w_out_odd):
    depth = lb_logits.shape[0]
    for layer in range(depth):
        j = layer // 2
        if layer % 2 == 0:
            x = _even_layer(x, norm_even[j], w_in_even[j], q_norm_even[j], k_norm_even[j], w_out_even[j])
        else:
            x = _odd_layer(x, norm_odd[j], w_in_odd[j], lb_logits, layer, o_norm_odd[j], w_out_odd[j])
    return x
```

```python
import functools
import math

import numpy as np
import jax
import jax.numpy as jnp
from jax import lax
from jax.experimental import pallas as pl
from jax.experimental.pallas import tpu as pltpu

HEAD_DIM = 128
RMS_EPS = 1e-6
DILATED_GROUPS = ((128, 1), (512, 4), (2048, 16))
DIL_BLOCK = 128
DIL_UNROLL = 32
DIL_INNER_STRIDE = 4
DIL_SCORE_BOUND = 60.0
SB_TILE = 256
SB_HEADS = 8
SB_LOG2_UNDERFLOW = -150.0
HGRN_CHUNK = 128
HGRN_LEVELS = 7
HGRN_HEADS = 32
HGRN_DIRECT_LOG2_MAX = 100.0
NEG_BIG = -1e30
VMEM_LIMIT_BYTES = 56 * 1024 * 1024

F32 = jnp.float32
BF16 = jnp.bfloat16
NT_DIMS = (((1,), (1,)), ((), ()))
TN_DIMS = (((0,), (0,)), ((), ()))


def _compiler_params(semantics):
    return pltpu.CompilerParams(dimension_semantics=semantics,
                                vmem_limit_bytes=VMEM_LIMIT_BYTES)


def _silu(x):
    return x * jax.nn.sigmoid(x)


def _rmsnorm_kernel(x_ref, g_ref, o_ref):
    x = x_ref[...]
    ms = jnp.mean(x * x, axis=-1, keepdims=True)
    o_ref[...] = (x * lax.rsqrt(ms + RMS_EPS) * g_ref[...]).astype(o_ref.dtype)


def _rmsnorm(x2d, gain, tm=512):
    m, d = x2d.shape
    return pl.pallas_call(
        _rmsnorm_kernel,
        grid=(m // tm,),
        in_specs=[pl.BlockSpec((tm, d), lambda i: (i, 0)),
                  pl.BlockSpec((1, d), lambda i: (0, 0))],
        out_specs=pl.BlockSpec((tm, d), lambda i: (i, 0)),
        out_shape=jax.ShapeDtypeStruct((m, d), BF16),
        compiler_params=_compiler_params(("parallel",)),
        name="rmsnorm",
    )(x2d, gain.reshape(1, d))


def _resident_weight(w_ref, wb_ref):
    @pl.when(pl.program_id(1) == 0)
    def _():
        wb_ref[...] = w_ref[...].astype(BF16)
    return wb_ref[...]


def _proj_kernel(a_ref, w_ref, o_ref, wb_ref):
    w = _resident_weight(w_ref, wb_ref)
    o_ref[...] = jnp.dot(a_ref[...], w, preferred_element_type=F32).astype(o_ref.dtype)


def _proj_scaled_kernel(a_ref, w_ref, s_ref, o_ref, wb_ref):
    w = _resident_weight(w_ref, wb_ref)
    acc = jnp.dot(a_ref[...], w, preferred_element_type=F32)
    o_ref[...] = (acc * s_ref[...]).astype(o_ref.dtype)


def _proj_headnorm_kernel(a_ref, w_ref, g_ref, o_ref, wb_ref):
    w = _resident_weight(w_ref, wb_ref)
    acc = jnp.dot(a_ref[...], w, preferred_element_type=F32)
    for h in range(acc.shape[1] // HEAD_DIM):
        cols = slice(h * HEAD_DIM, (h + 1) * HEAD_DIM)
        blk = acc[:, cols]
        ms = jnp.mean(blk * blk, axis=-1, keepdims=True)
        o_ref[:, cols] = (blk * lax.rsqrt(ms + RMS_EPS) * g_ref[:, cols]).astype(o_ref.dtype)


def _project(a, w, col0, n, out_dtype, head_gain=None, col_scale=None, tm=1024, tn=1024):
    assert head_gain is None or col_scale is None
    assert col0 % tn == 0 and n % tn == 0
    m, k = a.shape
    jt0 = col0 // tn
    in_specs = [pl.BlockSpec((tm, k), lambda j, i: (i, 0)),
                pl.BlockSpec((k, tn), lambda j, i: (0, jt0 + j))]
    args = [a, w]
    kern = _proj_kernel
    if head_gain is not None or col_scale is not None:
        in_specs.append(pl.BlockSpec((1, tn), lambda j, i: (0, j)))
        args.append(head_gain if head_gain is not None else col_scale)
        kern = _proj_headnorm_kernel if head_gain is not None else _proj_scaled_kernel
    return pl.pallas_call(
        kern,
        grid=(n // tn, m // tm),
        in_specs=in_specs,
        out_specs=pl.BlockSpec((tm, tn), lambda j, i: (i, j)),
        out_shape=jax.ShapeDtypeStruct((m, n), out_dtype),
        scratch_shapes=[pltpu.VMEM((k, tn), BF16)],
        compiler_params=_compiler_params(("parallel", "arbitrary")),
        name="project",
    )(*args)


def _out_proj_kernel(n_pairs, *refs):
    x_ref = refs[0]
    o_ref = refs[1 + 2 * n_pairs]
    acc = x_ref[...]
    for p in range(n_pairs):
        w = _resident_weight(refs[2 + 2 * p], refs[2 + 2 * n_pairs + p])
        acc = acc + jnp.dot(refs[1 + 2 * p][...], w, preferred_element_type=F32)
    o_ref[...] = acc


def _out_project(x2d, acts, w, tm=1024, tn=512):
    m, n = x2d.shape
    k = acts[0].shape[1]
    assert all(a.shape[1] == k for a in acts) and w.shape[0] == k * len(acts)
    in_specs = [pl.BlockSpec((tm, tn), lambda j, i: (i, j))]
    args = [x2d]
    for p, a in enumerate(acts):
        in_specs.append(pl.BlockSpec((tm, k), lambda j, i: (i, 0)))
        in_specs.append(pl.BlockSpec((k, tn), lambda j, i, p=p: (p, j)))
        args += [a, w]
    return pl.pallas_call(
        functools.partial(_out_proj_kernel, len(acts)),
        grid=(n // tn, m // tm),
        in_specs=in_specs,
        out_specs=pl.BlockSpec((tm, tn), lambda j, i: (i, j)),
        out_shape=jax.ShapeDtypeStruct((m, n), F32),
        scratch_shapes=[pltpu.VMEM((k, tn), BF16)] * len(acts),
        compiler_params=_compiler_params(("parallel", "arbitrary")),
        name="out_project",
    )(*args)


def _suffix_sum_matrix():
    t = HEAD_DIM
    sp = np.arange(t)[:, None]
    s = np.arange(t)[None, :]
    half = np.concatenate([(sp >= s).astype(np.float32), np.ones((t, t), np.float32)], axis=1)
    return jnp.asarray(np.concatenate([half, half], axis=0), dtype=BF16)


def _split_bf16(x):
    hi = x.astype(BF16)
    lo = (x - hi.astype(F32)).astype(BF16)
    return jnp.concatenate([hi, lo], axis=1)


def _sb_kernel(q_ref, k_ref, v_ref, g_ref, u_ref, o_ref, acc_ref, run_ref):
    t = SB_TILE
    hw = HEAD_DIM
    n_heads = q_ref.shape[1] // hw
    qi = pl.program_id(2)
    u = u_ref[...]
    row = lax.broadcasted_iota(jnp.int32, (t, t), 0)
    col = lax.broadcasted_iota(jnp.int32, (t, t), 1)
    causal = col < row

    def tile(j, diagonal):
        start = pl.multiple_of(j * t, t)
        logits = []
        split_parts = []
        for h in range(n_heads):
            cols = slice(h * hw, (h + 1) * hw)
            z = lax.dot_general(q_ref[:, cols], k_ref[pl.ds(start, t), cols], NT_DIMS,
                                preferred_element_type=F32)
            neg_z = -z
            l = jnp.minimum(neg_z, 0.0) - jnp.log2(1.0 + jnp.exp2(jnp.minimum(z, neg_z)))
            if diagonal:
                l = jnp.where(causal, l, 0.0)
            logits.append(z)
            split_parts += [_split_bf16(l[:, :hw]), _split_bf16(l[:, hw:])]
        sums = jnp.dot(jnp.concatenate(split_parts, axis=0), u, preferred_element_type=F32)
        for h in range(n_heads):
            cols = slice(h * hw, (h + 1) * hw)
            early = sums[(2 * h) * t:(2 * h + 1) * t]
            late = sums[(2 * h + 1) * t:(2 * h + 2) * t]
            after_late = 0.0 if diagonal else run_ref[:, cols]
            after_early = after_late + late[:, hw:]
            z = logits[h]
            a = jnp.exp2(jnp.concatenate([z[:, :hw] + early[:, :hw] + after_early,
                                          z[:, hw:] + late[:, :hw] + after_late], axis=1))
            if diagonal:
                a = jnp.where(causal, a, 0.0)
            pv = jnp.dot(a.astype(BF16), v_ref[pl.ds(start, t), cols], preferred_element_type=F32)
            acc_ref[:, cols] = pv if diagonal else acc_ref[:, cols] + pv
            run_ref[:, cols] = after_early + early[:, hw:]

    tile(qi, True)

    def more(carry):
        return (carry[0] < qi) & (carry[1] > SB_LOG2_UNDERFLOW)

    def body(carry):
        tile(qi - 1 - carry[0], False)
        return carry[0] + 1, jnp.max(run_ref[...])

    lax.while_loop(more, body, (0, jnp.max(run_ref[...])))
    o_ref[...] = (acc_ref[...] * _silu(g_ref[...].astype(F32))).astype(o_ref.dtype)


def _sb_attention(qkv, gate, n_heads):
    b, s, _ = qkv.shape
    t = SB_TILE
    assert n_heads % SB_HEADS == 0 and s % t == 0
    w = SB_HEADS * HEAD_DIM
    groups = n_heads // SB_HEADS
    return pl.pallas_call(
        _sb_kernel,
        grid=(b, groups, s // t),
        in_specs=[
            pl.BlockSpec((None, t, w), lambda bi, h, qi: (bi, qi, h)),
            pl.BlockSpec((None, s, w), lambda bi, h, qi: (bi, 0, groups + h)),
            pl.BlockSpec((None, s, w), lambda bi, h, qi: (bi, 0, 2 * groups + h)),
            pl.BlockSpec((None, t, w), lambda bi, h, qi: (bi, qi, h)),
            pl.BlockSpec((2 * HEAD_DIM, 2 * HEAD_DIM), lambda bi, h, qi: (0, 0)),
        ],
        out_specs=pl.BlockSpec((None, t, w), lambda bi, h, qi: (bi, qi, h)),
        out_shape=jax.ShapeDtypeStruct((b, s, n_heads * HEAD_DIM), BF16),
        scratch_shapes=[pltpu.VMEM((t, w), F32), pltpu.VMEM((t, w), F32)],
        compiler_params=_compiler_params(("parallel", "parallel", "arbitrary")),
        name="stickbreak_attention",
    )(qkv, qkv, qkv, gate, _suffix_sum_matrix())


def _dil_kernel(bounded_ref, q_ref, k_ref, v_ref, g_ref, *rest):
    s_len = q_ref.shape[0]
    blk = DIL_BLOCK
    inner = DIL_INNER_STRIDE
    outer = DILATED_GROUPS[2][1] // inner
    assert DILATED_GROUPS[0][1] == 1 and DILATED_GROUPS[1][1] == inner and outer == inner
    part = s_len // inner
    n_mid = part // blk
    n_wide = part // (outer * blk)
    qp, kp, vp = rest[:inner], rest[inner:2 * inner], rest[2 * inner:3 * inner]
    o_ref, ms_ref, ls_ref, as_ref, m_ref, l_ref, acc_ref = rest[3 * inner:]
    qi = lax.broadcasted_iota(jnp.int32, (blk, 2 * blk), 0)
    kj = lax.broadcasted_iota(jnp.int32, (blk, 2 * blk), 1)
    band = (kj >= qi) & (kj <= qi + blk)
    band_first = (lax.broadcasted_iota(jnp.int32, (blk, blk), 1)
                  <= lax.broadcasted_iota(jnp.int32, (blk, blk), 0))

    def scores(q, keys, valid):
        sc = lax.dot_general(q.astype(BF16), keys.astype(BF16), NT_DIMS, preferred_element_type=F32)
        return jnp.where(valid, sc, NEG_BIG)

    def weigh(sc, m_new, vals):
        halves = [sc[:, i:i + blk] for i in range(0, sc.shape[1], blk)]
        p = jnp.concatenate([jnp.exp(h if m_new is None else h - m_new) for h in halves], axis=1)
        pv = jnp.dot(p.astype(BF16),
                     jnp.concatenate([vals.astype(BF16), jnp.ones(vals.shape, BF16)], axis=1),
                     preferred_element_type=F32)
        return pv[:, :HEAD_DIM], pv[:, HEAD_DIM:]

    def update(sc, vals, old):
        m_old, l_old, acc_old = old
        if m_old is None:
            num, den = weigh(sc, None, vals)
            return None, l_old + den, acc_old + num
        m_new = jnp.maximum(m_old, jnp.max(sc, axis=-1, keepdims=True))
        num, den = weigh(sc, m_new, vals)
        alpha = jnp.exp(m_old - m_new)
        return m_new, alpha * l_old + den, alpha * acc_old + num

    def groups(track_max):
        for p in range(inner):
            for n in range(n_mid):
                cur = slice(n * blk, (n + 1) * blk)
                stat = slice(p * part + n * blk, p * part + (n + 1) * blk)
                if n == 0:
                    sc = scores(qp[p][cur, :], kp[p][cur, :], band_first)
                    vals = vp[p][cur, :]
                else:
                    both = slice((n - 1) * blk, (n + 1) * blk)
                    sc = scores(qp[p][cur, :], kp[p][both, :], band)
                    vals = vp[p][both, :]
                m_new = None
                if track_max:
                    m_new = jnp.broadcast_to(jnp.max(sc, axis=-1, keepdims=True), (blk, HEAD_DIM))
                    ms_ref[stat, :] = m_new
                num, den = weigh(sc, m_new, vals)
                ls_ref[stat, :] = den
                as_ref[stat, :] = num

        def wide_phase(p):
            loaded = []
            for ph in range(outer):
                for n in range(n_wide):
                    cur = pl.ds(ph + outer * blk * n, blk, stride=outer)
                    stat = pl.ds(p * part + ph + outer * blk * n, blk, stride=outer)
                    if n == 0:
                        sc = scores(qp[p][cur, :], kp[p][cur, :], band_first)
                        vals = vp[p][cur, :]
                    else:
                        prev = pl.ds(ph + outer * blk * (n - 1), blk, stride=outer)
                        sc = scores(qp[p][cur, :],
                                    jnp.concatenate([kp[p][prev, :], kp[p][cur, :]], axis=0), band)
                        vals = jnp.concatenate([vp[p][prev, :], vp[p][cur, :]], axis=0)
                    m_old = ms_ref[stat, :] if track_max else None
                    loaded.append((stat, sc, vals, (m_old, ls_ref[stat, :], as_ref[stat, :])))
            results = [(stat,) + update(sc, vals, old) for stat, sc, vals, old in loaded]
            for stat, m_new, l_new, acc_new in results:
                if track_max:
                    ms_ref[stat, :] = m_new
                ls_ref[stat, :] = l_new
                as_ref[stat, :] = acc_new

        for p in range(inner):
            wide_phase(p)

        def merge(i, carry):
            for p in range(inner):
                src = pl.ds(pl.multiple_of(p * part + blk * i, blk), blk)
                dst = pl.ds(p + inner * blk * i, blk, stride=inner)
                if track_max:
                    m_ref[dst, :] = ms_ref[src, :]
                l_ref[dst, :] = ls_ref[src, :]
                acc_ref[dst, :] = as_ref[src, :]
            return carry

        lax.fori_loop(0, part // blk, merge, 0)

        def dense_group(it, carry):
            loaded = []
            for u in range(DIL_UNROLL):
                n = it * DIL_UNROLL + u
                cur = pl.ds(pl.multiple_of(n * blk, blk), blk)
                prev = pl.ds(pl.multiple_of(jnp.maximum(n - 1, 0) * blk, blk), blk)
                keys = jnp.concatenate([k_ref[prev, :], k_ref[cur, :]], axis=0)
                vals = jnp.concatenate([v_ref[prev, :], v_ref[cur, :]], axis=0)
                sc = scores(q_ref[cur, :], keys, band & ((kj >= blk) | (n > 0)))
                m_old = m_ref[cur, :] if track_max else None
                loaded.append((cur, sc, vals, (m_old, l_ref[cur, :], acc_ref[cur, :])))
            for cur, sc, vals, old in loaded:
                _, l_new, acc_new = update(sc, vals, old)
                o_ref[cur, :] = (acc_new / l_new * _silu(g_ref[cur, :].astype(F32))).astype(o_ref.dtype)
            return carry

        lax.fori_loop(0, s_len // (blk * DIL_UNROLL), dense_group, 0)

    bounded = bounded_ref[0] != 0
    pl.when(bounded)(lambda: groups(False))
    pl.when(jnp.logical_not(bounded))(lambda: groups(True))


def _dilated_attention(qk, v, gate, n_heads, gate_col0, q_gain, k_gain):
    b, s, _ = v.shape
    inner = DIL_INNER_STRIDE
    score_bound = math.sqrt(HEAD_DIM) * jnp.max(jnp.abs(q_gain)) * jnp.max(jnp.abs(k_gain))
    bounded = (score_bound <= DIL_SCORE_BOUND).astype(jnp.int32).reshape(1)
    seq = lambda off: pl.BlockSpec((None, s, HEAD_DIM), lambda bi, h: (bi, 0, off + h))
    phase = lambda ncols, off: [
        pl.BlockSpec((None, s // inner, HEAD_DIM), lambda bi, h, p=p: (bi, 0, p * ncols + off + h))
        for p in range(inner)]
    qk_split = qk.reshape(b, s // inner, inner * 2 * n_heads * HEAD_DIM)
    v_split = v.reshape(b, s // inner, inner * n_heads * HEAD_DIM)
    return pl.pallas_call(
        _dil_kernel,
        grid=(b, n_heads),
        in_specs=([pl.BlockSpec(memory_space=pltpu.SMEM), seq(0), seq(n_heads), seq(0), seq(gate_col0)]
                  + phase(2 * n_heads, 0) + phase(2 * n_heads, n_heads) + phase(n_heads, 0)),
        out_specs=seq(0),
        out_shape=jax.ShapeDtypeStruct((b, s, n_heads * HEAD_DIM), BF16),
        scratch_shapes=[pltpu.VMEM((s, HEAD_DIM), F32)] * 6,
        compiler_params=_compiler_params(("parallel", "parallel")),
        name="dilated_attention",
    )(bounded, qk, qk, v, gate, *([qk_split] * (2 * inner)), *([v_split] * inner))


def _hgrn_sum_matrix():
    c = HGRN_CHUNK
    t = np.arange(c)[:, None]
    j = np.arange(c)[None, :]
    blocks = [(j <= t), (j > t)]
    for lev in range(HGRN_LEVELS):
        m = 1 << lev
        mid = (t // (2 * m)) * (2 * m) + m - 1
        second = ((t // m) % 2) == 1
        blocks.append(np.where(second, (j > mid) & (j <= t), (j > t) & (j <= mid)))
    mat = np.concatenate(blocks, axis=0).astype(np.float32)
    return jnp.asarray(np.concatenate([mat, mat], axis=1), dtype=BF16)


def _hgrn_kernel(layer, q_ref, f_ref, i_ref, g_ref, lbl_ref, og_ref, sm_ref, o_ref, st_ref):
    c = HGRN_CHUNK
    hw = HEAD_DIM
    n_heads = q_ref.shape[1] // hw

    @pl.when(pl.program_id(2) == 0)
    def _():
        st_ref[...] = jnp.zeros_like(st_ref)

    lbl = lbl_ref[...]
    e = jnp.exp(lbl - jnp.max(lbl, axis=0, keepdims=True))
    share = e / jnp.sum(e, axis=0, keepdims=True)
    lb_all = jnp.sum(share[:layer + 1], axis=0, keepdims=True) - share[0:1]

    row = lax.broadcasted_iota(jnp.int32, (c, c), 0)
    col = lax.broadcasted_iota(jnp.int32, (c, c), 1)

    keys = []
    his = []
    los = []
    totals = []
    for h in range(n_heads):
        cols = slice(h * hw, (h + 1) * hw)
        lb = lb_all[:, cols]
        f = lb + (1.0 - lb) * jax.nn.sigmoid(f_ref[:, cols])
        keys.append(1.0 - f)
        log2f = jnp.log2(f)
        totals.append(jnp.sum(log2f, axis=0, keepdims=True))
        hi = log2f.astype(BF16)
        his.append(hi)
        los.append((log2f - hi.astype(F32)).astype(BF16))
    rhs = jnp.concatenate([jnp.concatenate(his, axis=1), jnp.concatenate(los, axis=1)], axis=0)
    prefix_all = jnp.dot(sm_ref[0:c, :], rhs, preferred_element_type=F32)
    direct_ok = jnp.min(jnp.concatenate(totals, axis=1)) >= -HGRN_DIRECT_LOG2_MAX

    def finish(h, out, state, inp, key_tail, last_decay):
        cols = slice(h * hw, (h + 1) * hw)
        st_ref[h] = state * last_decay + lax.dot_general(inp, key_tail, TN_DIMS, preferred_element_type=F32)
        ms = jnp.mean(out * out, axis=-1, keepdims=True)
        o_ref[:, cols] = (out * lax.rsqrt(ms + RMS_EPS) * og_ref[...]
                          * _silu(g_ref[:, cols].astype(F32))).astype(o_ref.dtype)

    @pl.when(direct_ok)
    def _():
        keep = col <= row
        for h in range(n_heads):
            cols = slice(h * hw, (h + 1) * hw)
            prefix = prefix_all[:, cols]
            grow = jnp.exp2(-prefix)
            q_dec = (q_ref[:, cols].astype(F32) * jnp.exp2(prefix)).astype(BF16)
            key_grow = keys[h] * grow
            inp = i_ref[:, cols]
            state = st_ref[h]
            attn = jnp.where(keep, lax.dot_general(q_dec, key_grow.astype(BF16), NT_DIMS,
                                                   preferred_element_type=F32), 0.0)
            out = (lax.dot_general(q_dec, state.astype(BF16), NT_DIMS, preferred_element_type=F32)
                   + jnp.dot(attn.astype(BF16), inp, preferred_element_type=F32))
            last_decay = jnp.exp2(prefix[c - 1:c, :])
            finish(h, out, state, inp, (key_grow * last_decay).astype(BF16), last_decay)

    @pl.when(jnp.logical_not(direct_ok))
    def _():
        lower = col < row
        split_bit = row ^ col
        level_masks = [lower & ((split_bit >> lev) == 1) for lev in range(HGRN_LEVELS)]
        sums_all = jnp.dot(sm_ref[c:, :], rhs, preferred_element_type=F32)
        for h in range(n_heads):
            cols = slice(h * hw, (h + 1) * hw)
            sums = sums_all[:, cols]
            prefix = prefix_all[:, cols]
            key = keys[h]
            q = q_ref[:, cols].astype(F32)
            inp = i_ref[:, cols]
            state = st_ref[h]
            out = lax.dot_general((q * jnp.exp2(prefix)).astype(BF16), state.astype(BF16), NT_DIMS,
                                  preferred_element_type=F32)
            attn = jnp.where(row == col,
                             lax.dot_general(q.astype(BF16), key.astype(BF16), NT_DIMS,
                                             preferred_element_type=F32), 0.0)
            for lev in range(HGRN_LEVELS):
                w = jnp.exp2(sums[(1 + lev) * c:(2 + lev) * c])
                pair = lax.dot_general((q * w).astype(BF16), (key * w).astype(BF16), NT_DIMS,
                                       preferred_element_type=F32)
                attn = jnp.where(level_masks[lev], pair, attn)
            out = out + jnp.dot(attn.astype(BF16), inp, preferred_element_type=F32)
            finish(h, out, state, inp, (key * jnp.exp2(sums[0:c])).astype(BF16),
                   jnp.exp2(prefix[c - 1:c, :]))


def _hgrn(q, f_pre, inp_gate, lb_logits, o_gain, layer, n_heads):
    b, s, _ = q.shape
    c = HGRN_CHUNK
    assert n_heads % HGRN_HEADS == 0 and s % c == 0
    w = HGRN_HEADS * HEAD_DIM
    groups = n_heads // HGRN_HEADS
    depth = lb_logits.shape[0]
    tok = lambda off: pl.BlockSpec((None, c, w), lambda bi, h, ci: (bi, ci, off + h))
    return pl.pallas_call(
        functools.partial(_hgrn_kernel, layer),
        grid=(b, groups, s // c),
        in_specs=[tok(0), tok(0), tok(0), tok(groups),
                  pl.BlockSpec((depth, w), lambda bi, h, ci: (0, h)),
                  pl.BlockSpec((1, HEAD_DIM), lambda bi, h, ci: (0, 0)),
                  pl.BlockSpec(((2 + HGRN_LEVELS) * c, 2 * c), lambda bi, h, ci: (0, 0))],
        out_specs=tok(0),
        out_shape=jax.ShapeDtypeStruct((b, s, n_heads * HEAD_DIM), BF16),
        scratch_shapes=[pltpu.VMEM((HGRN_HEADS, HEAD_DIM, HEAD_DIM), F32)],
        compiler_params=_compiler_params(("parallel", "parallel", "arbitrary")),
        name="hgrn2",
    )(q, f_pre, inp_gate, inp_gate, lb_logits, o_gain.reshape(1, HEAD_DIM), _hgrn_sum_matrix())


def _even_layer(x, norm_g, w_in, q_gain, k_gain, w_out):
    b, s, d = x.shape
    d_inner = w_out.shape[0]
    d_half = d_inner // 2
    n_heads = d_half // HEAD_DIM
    m = b * s
    x2d = x.reshape(m, d)
    hdn = _rmsnorm(x2d, norm_g)
    c_sb, c_dil_qk, c_dil_v = 3 * d_half, 5 * d_half, 6 * d_half
    q_scale = math.log2(math.e) / math.sqrt(HEAD_DIM)
    sb_scale = jnp.concatenate([jnp.full((1, d_half), q_scale, F32), jnp.ones((1, 2 * d_half), F32)], axis=1)
    qkv_sb = _project(hdn, w_in, 0, c_sb, BF16, col_scale=sb_scale)
    head_gain = jnp.concatenate([jnp.tile(q_gain / math.sqrt(HEAD_DIM), n_heads),
                                 jnp.tile(k_gain, n_heads)]).reshape(1, 2 * d_half)
    qk_dil = _project(hdn, w_in, c_sb, c_dil_qk - c_sb, F32, head_gain=head_gain)
    v_dil = _project(hdn, w_in, c_dil_qk, c_dil_v - c_dil_qk, F32)
    gate = _project(hdn, w_in, c_dil_v, d_inner, BF16).reshape(b, s, d_inner)
    mix_sb = _sb_attention(qkv_sb.reshape(b, s, c_sb), gate, n_heads)
    mix_dil = _dilated_attention(qk_dil.reshape(b, s, 2 * d_half), v_dil.reshape(b, s, d_half),
                                 gate, n_heads, n_heads, q_gain, k_gain)
    y = _out_project(x2d, [mix_sb.reshape(m, d_half), mix_dil.reshape(m, d_half)], w_out)
    return y.reshape(b, s, d)


def _odd_layer(x, norm_g, w_in, lb_logits, layer, o_gain, w_out):
    b, s, d = x.shape
    d_inner = w_out.shape[0]
    n_heads = d_inner // HEAD_DIM
    m = b * s
    x2d = x.reshape(m, d)
    hdn = _rmsnorm(x2d, norm_g)
    q = _project(hdn, w_in, 0, d_inner, BF16)
    f_pre = _project(hdn, w_in, d_inner, d_inner, F32)
    inp_gate = _project(hdn, w_in, 2 * d_inner, 2 * d_inner, BF16)
    mix = _hgrn(q.reshape(b, s, d_inner), f_pre.reshape(b, s, d_inner),
                inp_gate.reshape(b, s, 2 * d_inner), lb_logits, o_gain, layer, n_heads)
    y = _out_project(x2d, [mix.reshape(m, d_inner)], w_out)
    return y.reshape(b, s, d)


def kernel(x, norm_even, w_in_even, q_norm_even, k_norm_even, w_out_even,
           norm_odd, w_in_odd, lb_logits, o_norm_odd, w_out_odd):
    depth = lb_logits.shape[0]
    for layer in range(depth):
        j = layer // 2
        if layer % 2 == 0:
            x = _even_layer(x, norm_even[j], w_in_even[j], q_norm_even[j], k_norm_even[j], w_out_even[j])
        else:
            x = _odd_layer(x, norm_odd[j], w_in_odd[j], lb_logits, layer, o_norm_odd[j], w_out_odd[j])
    return x
```

```python
import functools
import math

import numpy as np
import jax
import jax.numpy as jnp
from jax import lax
from jax.experimental import pallas as pl
from jax.experimental.pallas import tpu as pltpu

HEAD_DIM = 128
RMS_EPS = 1e-6
DILATED_GROUPS = ((128, 1), (512, 4), (2048, 16))
DIL_BLOCK = 128
DIL_UNROLL = 32
DIL_MID_PHASES_PER_ITER = 4
DIL_WIDE_PHASES_PER_ITER = 1
DIL_INNER_STRIDE = 4
DIL_SCORE_BOUND = 60.0
SB_TILE = 256
SB_HEADS = 8
SB_LOG2_UNDERFLOW = -150.0
HGRN_CHUNK = 128
HGRN_LEVELS = 7
HGRN_HEADS = 32
HGRN_DIRECT_LOG2_MAX = 100.0
NEG_BIG = -1e30
VMEM_LIMIT_BYTES = 56 * 1024 * 1024

F32 = jnp.float32
BF16 = jnp.bfloat16
NT_DIMS = (((1,), (1,)), ((), ()))
TN_DIMS = (((0,), (0,)), ((), ()))


def _compiler_params(semantics):
    return pltpu.CompilerParams(dimension_semantics=semantics,
                                vmem_limit_bytes=VMEM_LIMIT_BYTES)


def _silu(x):
    return x * jax.nn.sigmoid(x)


def _rmsnorm_kernel(x_ref, g_ref, o_ref):
    x = x_ref[...]
    ms = jnp.mean(x * x, axis=-1, keepdims=True)
    o_ref[...] = (x * lax.rsqrt(ms + RMS_EPS) * g_ref[...]).astype(o_ref.dtype)


def _rmsnorm(x2d, gain, tm=512):
    m, d = x2d.shape
    return pl.pallas_call(
        _rmsnorm_kernel,
        grid=(m // tm,),
        in_specs=[pl.BlockSpec((tm, d), lambda i: (i, 0)),
                  pl.BlockSpec((1, d), lambda i: (0, 0))],
        out_specs=pl.BlockSpec((tm, d), lambda i: (i, 0)),
        out_shape=jax.ShapeDtypeStruct((m, d), BF16),
        compiler_params=_compiler_params(("parallel",)),
        name="rmsnorm",
    )(x2d, gain.reshape(1, d))


def _resident_weight(w_ref, wb_ref):
    @pl.when(pl.program_id(1) == 0)
    def _():
        wb_ref[...] = w_ref[...].astype(BF16)
    return wb_ref[...]


def _proj_kernel(a_ref, w_ref, o_ref, wb_ref):
    w = _resident_weight(w_ref, wb_ref)
    o_ref[...] = jnp.dot(a_ref[...], w, preferred_element_type=F32).astype(o_ref.dtype)


def _proj_silu_kernel(a_ref, w_ref, o_ref, wb_ref):
    w = _resident_weight(w_ref, wb_ref)
    o_ref[...] = _silu(jnp.dot(a_ref[...], w, preferred_element_type=F32)).astype(o_ref.dtype)


def _proj_scaled_kernel(a_ref, w_ref, s_ref, o_ref, wb_ref):
    w = _resident_weight(w_ref, wb_ref)
    acc = jnp.dot(a_ref[...], w, preferred_element_type=F32)
    o_ref[...] = (acc * s_ref[...]).astype(o_ref.dtype)


def _proj_headnorm_kernel(a_ref, w_ref, g_ref, o_ref, wb_ref):
    w = _resident_weight(w_ref, wb_ref)
    acc = jnp.dot(a_ref[...], w, preferred_element_type=F32)
    for h in range(acc.shape[1] // HEAD_DIM):
        cols = slice(h * HEAD_DIM, (h + 1) * HEAD_DIM)
        blk = acc[:, cols]
        ms = jnp.mean(blk * blk, axis=-1, keepdims=True)
        o_ref[:, cols] = (blk * lax.rsqrt(ms + RMS_EPS) * g_ref[:, cols]).astype(o_ref.dtype)


def _project(a, w, col0, n, out_dtype, head_gain=None, col_scale=None, silu=False, tm=1024, tn=1024):
    assert (head_gain is not None) + (col_scale is not None) + silu <= 1
    assert col0 % tn == 0 and n % tn == 0
    m, k = a.shape
    jt0 = col0 // tn
    in_specs = [pl.BlockSpec((tm, k), lambda j, i: (i, 0)),
                pl.BlockSpec((k, tn), lambda j, i: (0, jt0 + j))]
    args = [a, w]
    kern = _proj_silu_kernel if silu else _proj_kernel
    if head_gain is not None or col_scale is not None:
        in_specs.append(pl.BlockSpec((1, tn), lambda j, i: (0, j)))
        args.append(head_gain if head_gain is not None else col_scale)
        kern = _proj_headnorm_kernel if head_gain is not None else _proj_scaled_kernel
    return pl.pallas_call(
        kern,
        grid=(n // tn, m // tm),
        in_specs=in_specs,
        out_specs=pl.BlockSpec((tm, tn), lambda j, i: (i, j)),
        out_shape=jax.ShapeDtypeStruct((m, n), out_dtype),
        scratch_shapes=[pltpu.VMEM((k, tn), BF16)],
        compiler_params=_compiler_params(("parallel", "arbitrary")),
        name="project",
    )(*args)


def _out_proj_kernel(n_pairs, *refs):
    x_ref = refs[0]
    o_ref = refs[1 + 2 * n_pairs]
    acc = x_ref[...]
    for p in range(n_pairs):
        w = _resident_weight(refs[2 + 2 * p], refs[2 + 2 * n_pairs + p])
        acc = acc + jnp.dot(refs[1 + 2 * p][...], w, preferred_element_type=F32)
    o_ref[...] = acc


def _out_project(x2d, acts, w, tm=1024, tn=512):
    m, n = x2d.shape
    k = acts[0].shape[1]
    assert all(a.shape[1] == k for a in acts) and w.shape[0] == k * len(acts)
    in_specs = [pl.BlockSpec((tm, tn), lambda j, i: (i, j))]
    args = [x2d]
    for p, a in enumerate(acts):
        in_specs.append(pl.BlockSpec((tm, k), lambda j, i: (i, 0)))
        in_specs.append(pl.BlockSpec((k, tn), lambda j, i, p=p: (p, j)))
        args += [a, w]
    return pl.pallas_call(
        functools.partial(_out_proj_kernel, len(acts)),
        grid=(n // tn, m // tm),
        in_specs=in_specs,
        out_specs=pl.BlockSpec((tm, tn), lambda j, i: (i, j)),
        out_shape=jax.ShapeDtypeStruct((m, n), F32),
        scratch_shapes=[pltpu.VMEM((k, tn), BF16)] * len(acts),
        compiler_params=_compiler_params(("parallel", "arbitrary")),
        name="out_project",
    )(*args)


def _suffix_sum_matrix():
    t = HEAD_DIM
    sp = np.arange(t)[:, None]
    s = np.arange(t)[None, :]
    half = np.concatenate([(sp >= s).astype(np.float32), np.ones((t, t), np.float32)], axis=1)
    return jnp.asarray(np.concatenate([half, half], axis=0), dtype=BF16)


def _split_bf16(x):
    hi = x.astype(BF16)
    lo = (x - hi.astype(F32)).astype(BF16)
    return jnp.concatenate([hi, lo], axis=1)


def _sb_kernel(q_ref, k_ref, v_ref, g_ref, u_ref, o_ref, acc_ref, run_ref):
    t = SB_TILE
    hw = HEAD_DIM
    n_heads = q_ref.shape[1] // hw
    qi = pl.program_id(2)
    u = u_ref[...]
    row = lax.broadcasted_iota(jnp.int32, (t, t), 0)
    col = lax.broadcasted_iota(jnp.int32, (t, t), 1)
    causal = col < row

    def tile(j, diagonal):
        start = pl.multiple_of(j * t, t)
        logits = []
        split_parts = []
        for h in range(n_heads):
            cols = slice(h * hw, (h + 1) * hw)
            z = lax.dot_general(q_ref[:, cols], k_ref[pl.ds(start, t), cols], NT_DIMS,
                                preferred_element_type=F32)
            neg_z = -z
            l = jnp.minimum(neg_z, 0.0) - jnp.log2(1.0 + jnp.exp2(jnp.minimum(z, neg_z)))
            if diagonal:
                l = jnp.where(causal, l, 0.0)
            logits.append(z)
            split_parts += [_split_bf16(l[:, :hw]), _split_bf16(l[:, hw:])]
        sums = jnp.dot(jnp.concatenate(split_parts, axis=0), u, preferred_element_type=F32)
        for h in range(n_heads):
            cols = slice(h * hw, (h + 1) * hw)
            early = sums[(2 * h) * t:(2 * h + 1) * t]
            late = sums[(2 * h + 1) * t:(2 * h + 2) * t]
            after_late = 0.0 if diagonal else run_ref[:, cols]
            after_early = after_late + late[:, hw:]
            z = logits[h]
            a = jnp.exp2(jnp.concatenate([z[:, :hw] + early[:, :hw] + after_early,
                                          z[:, hw:] + late[:, :hw] + after_late], axis=1))
            if diagonal:
                a = jnp.where(causal, a, 0.0)
            pv = jnp.dot(a.astype(BF16), v_ref[pl.ds(start, t), cols], preferred_element_type=F32)
            acc_ref[:, cols] = pv if diagonal else acc_ref[:, cols] + pv
            run_ref[:, cols] = after_early + early[:, hw:]

    tile(qi, True)

    def more(carry):
        return (carry[0] < qi) & (carry[1] > SB_LOG2_UNDERFLOW)

    def body(carry):
        tile(qi - 1 - carry[0], False)
        return carry[0] + 1, jnp.max(run_ref[...])

    lax.while_loop(more, body, (0, jnp.max(run_ref[...])))
    o_ref[...] = (acc_ref[...] * g_ref[...].astype(F32)).astype(o_ref.dtype)


def _sb_attention(qkv, gate, n_heads):
    b, s, _ = qkv.shape
    t = SB_TILE
    assert n_heads % SB_HEADS == 0 and s % t == 0
    w = SB_HEADS * HEAD_DIM
    groups = n_heads // SB_HEADS
    return pl.pallas_call(
        _sb_kernel,
        grid=(b, groups, s // t),
        in_specs=[
            pl.BlockSpec((None, t, w), lambda bi, h, qi: (bi, qi, h)),
            pl.BlockSpec((None, s, w), lambda bi, h, qi: (bi, 0, groups + h)),
            pl.BlockSpec((None, s, w), lambda bi, h, qi: (bi, 0, 2 * groups + h)),
            pl.BlockSpec((None, t, w), lambda bi, h, qi: (bi, qi, h)),
            pl.BlockSpec((2 * HEAD_DIM, 2 * HEAD_DIM), lambda bi, h, qi: (0, 0)),
        ],
        out_specs=pl.BlockSpec((None, t, w), lambda bi, h, qi: (bi, qi, h)),
        out_shape=jax.ShapeDtypeStruct((b, s, n_heads * HEAD_DIM), BF16),
        scratch_shapes=[pltpu.VMEM((t, w), F32), pltpu.VMEM((t, w), F32)],
        compiler_params=_compiler_params(("parallel", "parallel", "arbitrary")),
        name="stickbreak_attention",
    )(qkv, qkv, qkv, gate, _suffix_sum_matrix())


def _dil_kernel(bounded_ref, q_ref, k_ref, v_ref, g_ref, o_ref,
                qs_ref, ks_ref, vs_ref, ms_ref, ls_ref, as_ref):
    s_len = q_ref.shape[0]
    blk = DIL_BLOCK
    inner = DIL_INNER_STRIDE
    outer = DILATED_GROUPS[2][1] // inner
    assert DILATED_GROUPS[0][1] == 1 and DILATED_GROUPS[1][1] == inner and outer == inner
    part = s_len // inner
    n_mid = part // blk
    n_wide = part // (outer * blk)
    scale = 1.0 / math.sqrt(HEAD_DIM)
    qi = lax.broadcasted_iota(jnp.int32, (blk, 2 * blk), 0)
    kj = lax.broadcasted_iota(jnp.int32, (blk, 2 * blk), 1)
    band = (kj >= qi) & (kj <= qi + blk)
    band_first = (lax.broadcasted_iota(jnp.int32, (blk, blk), 1)
                  <= lax.broadcasted_iota(jnp.int32, (blk, blk), 0))

    def scores(q, keys, valid):
        sc = lax.dot_general(q.astype(BF16), keys.astype(BF16), NT_DIMS, preferred_element_type=F32)
        return jnp.where(valid, sc, NEG_BIG)

    def weigh(sc, m_new, vals):
        halves = [sc[:, i:i + blk] for i in range(0, sc.shape[1], blk)]
        p = jnp.concatenate([jnp.exp(h if m_new is None else h - m_new) for h in halves], axis=1)
        pv = jnp.dot(p.astype(BF16),
                     jnp.concatenate([vals.astype(BF16), jnp.ones(vals.shape, BF16)], axis=1),
                     preferred_element_type=F32)
        return pv[:, :HEAD_DIM], pv[:, HEAD_DIM:]

    def update(sc, vals, old):
        m_old, l_old, acc_old = old
        if m_old is None:
            num, den = weigh(sc, None, vals)
            return None, l_old + den, acc_old + num
        m_new = jnp.maximum(m_old, jnp.max(sc, axis=-1, keepdims=True))
        num, den = weigh(sc, m_new, vals)
        alpha = jnp.exp(m_old - m_new)
        return m_new, alpha * l_old + den, alpha * acc_old + num

    def split(i, carry):
        for p in range(inner):
            src = pl.ds(p + inner * blk * i, blk, stride=inner)
            dst = pl.ds(pl.multiple_of(p * part + blk * i, blk), blk)
            qs_ref[dst, :] = q_ref[src, :] * scale
            ks_ref[dst, :] = k_ref[src, :]
            vs_ref[dst, :] = v_ref[src, :]
        return carry

    lax.fori_loop(0, part // blk, split, 0)

    def groups(track_max):
        def mid_group(it, carry):
            for pp in range(DIL_MID_PHASES_PER_ITER):
                base = pl.multiple_of((it * DIL_MID_PHASES_PER_ITER + pp) * part, blk)
                for n in range(n_mid):
                    cur = pl.ds(base + n * blk, blk)
                    if n == 0:
                        sc = scores(qs_ref[cur, :], ks_ref[cur, :], band_first)
                        vals = vs_ref[cur, :]
                    else:
                        both = pl.ds(base + (n - 1) * blk, 2 * blk)
                        sc = scores(qs_ref[cur, :], ks_ref[both, :], band)
                        vals = vs_ref[both, :]
                    m_new = None
                    if track_max:
                        m_new = jnp.broadcast_to(jnp.max(sc, axis=-1, keepdims=True), (blk, HEAD_DIM))
                        ms_ref[cur, :] = m_new
                    num, den = weigh(sc, m_new, vals)
                    ls_ref[cur, :] = den
                    as_ref[cur, :] = num
            return carry

        lax.fori_loop(0, inner // DIL_MID_PHASES_PER_ITER, mid_group, 0)

        def wide_group(it, carry):
            loaded = []
            for pp in range(DIL_WIDE_PHASES_PER_ITER):
                base = (it * DIL_WIDE_PHASES_PER_ITER + pp) * part
                for ph in range(outer):
                    for n in range(n_wide):
                        cur = pl.ds(base + ph + outer * blk * n, blk, stride=outer)
                        if n == 0:
                            sc = scores(qs_ref[cur, :], ks_ref[cur, :], band_first)
                            vals = vs_ref[cur, :]
                        else:
                            prev = pl.ds(base + ph + outer * blk * (n - 1), blk, stride=outer)
                            sc = scores(qs_ref[cur, :],
                                        jnp.concatenate([ks_ref[prev, :], ks_ref[cur, :]], axis=0), band)
                            vals = jnp.concatenate([vs_ref[prev, :], vs_ref[cur, :]], axis=0)
                        m_old = ms_ref[cur, :] if track_max else None
                        loaded.append((cur, sc, vals, (m_old, ls_ref[cur, :], as_ref[cur, :])))
            results = [(cur,) + update(sc, vals, old) for cur, sc, vals, old in loaded]
            for cur, m_new, l_new, acc_new in results:
                if track_max:
                    ms_ref[cur, :] = m_new
                ls_ref[cur, :] = l_new
                as_ref[cur, :] = acc_new
            return carry

        lax.fori_loop(0, inner // DIL_WIDE_PHASES_PER_ITER, wide_group, 0)

        m_ref, l_ref, acc_ref = qs_ref, ks_ref, vs_ref

        def merge(i, carry):
            for p in range(inner):
                src = pl.ds(pl.multiple_of(p * part + blk * i, blk), blk)
                dst = pl.ds(p + inner * blk * i, blk, stride=inner)
                if track_max:
                    m_ref[dst, :] = ms_ref[src, :]
                l_ref[dst, :] = ls_ref[src, :]
                acc_ref[dst, :] = as_ref[src, :]
            return carry

        lax.fori_loop(0, part // blk, merge, 0)

        def dense_group(it, carry):
            loaded = []
            for u in range(DIL_UNROLL):
                n = it * DIL_UNROLL + u
                cur = pl.ds(pl.multiple_of(n * blk, blk), blk)
                prev = pl.ds(pl.multiple_of(jnp.maximum(n - 1, 0) * blk, blk), blk)
                keys = jnp.concatenate([k_ref[prev, :], k_ref[cur, :]], axis=0)
                vals = jnp.concatenate([v_ref[prev, :], v_ref[cur, :]], axis=0)
                sc = scores(q_ref[cur, :] * scale, keys, band & ((kj >= blk) | (n > 0)))
                m_old = m_ref[cur, :] if track_max else None
                loaded.append((cur, sc, vals, (m_old, l_ref[cur, :], acc_ref[cur, :])))
            for cur, sc, vals, old in loaded:
                _, l_new, acc_new = update(sc, vals, old)
                o_ref[cur, :] = (acc_new / l_new * g_ref[cur, :].astype(F32)).astype(o_ref.dtype)
            return carry

        lax.fori_loop(0, s_len // (blk * DIL_UNROLL), dense_group, 0)

    bounded = bounded_ref[0] != 0
    pl.when(bounded)(lambda: groups(False))
    pl.when(jnp.logical_not(bounded))(lambda: groups(True))


def _dilated_attention(qk, v, gate, n_heads, gate_col0, q_gain, k_gain):
    b, s, _ = v.shape
    score_bound = math.sqrt(HEAD_DIM) * jnp.max(jnp.abs(q_gain)) * jnp.max(jnp.abs(k_gain))
    bounded = (score_bound <= DIL_SCORE_BOUND).astype(jnp.int32).reshape(1)
    seq = lambda off: pl.BlockSpec((None, s, HEAD_DIM), lambda bi, h: (bi, 0, off + h))
    return pl.pallas_call(
        _dil_kernel,
        grid=(b, n_heads),
        in_specs=[pl.BlockSpec(memory_space=pltpu.SMEM), seq(0), seq(n_heads), seq(0), seq(gate_col0)],
        out_specs=seq(0),
        out_shape=jax.ShapeDtypeStruct((b, s, n_heads * HEAD_DIM), BF16),
        scratch_shapes=[pltpu.VMEM((s, HEAD_DIM), F32)] * 6,
        compiler_params=_compiler_params(("parallel", "parallel")),
        name="dilated_attention",
    )(bounded, qk, qk, v, gate)


def _hgrn_sum_matrix():
    c = HGRN_CHUNK
    t = np.arange(c)[:, None]
    j = np.arange(c)[None, :]
    blocks = [(j <= t), (j > t)]
    for lev in range(HGRN_LEVELS):
        m = 1 << lev
        mid = (t // (2 * m)) * (2 * m) + m - 1
        second = ((t // m) % 2) == 1
        blocks.append(np.where(second, (j > mid) & (j <= t), (j > t) & (j <= mid)))
    mat = np.concatenate(blocks, axis=0).astype(np.float32)
    return jnp.asarray(np.concatenate([mat, mat], axis=1), dtype=BF16)


def _hgrn_kernel(layer, q_ref, f_ref, i_ref, g_ref, lbl_ref, og_ref, sm_ref, o_ref, st_ref):
    c = HGRN_CHUNK
    hw = HEAD_DIM
    n_heads = q_ref.shape[1] // hw

    @pl.when(pl.program_id(2) == 0)
    def _():
        st_ref[...] = jnp.zeros_like(st_ref)

    lbl = lbl_ref[...]
    e = jnp.exp(lbl - jnp.max(lbl, axis=0, keepdims=True))
    share = e / jnp.sum(e, axis=0, keepdims=True)
    lb_all = jnp.sum(share[:layer + 1], axis=0, keepdims=True) - share[0:1]

    row = lax.broadcasted_iota(jnp.int32, (c, c), 0)
    col = lax.broadcasted_iota(jnp.int32, (c, c), 1)

    keys = []
    his = []
    los = []
    totals = []
    for h in range(n_heads):
        cols = slice(h * hw, (h + 1) * hw)
        lb = lb_all[:, cols]
        f = lb + (1.0 - lb) * jax.nn.sigmoid(f_ref[:, cols])
        keys.append(1.0 - f)
        log2f = jnp.log2(f)
        totals.append(jnp.sum(log2f, axis=0, keepdims=True))
        hi = log2f.astype(BF16)
        his.append(hi)
        los.append((log2f - hi.astype(F32)).astype(BF16))
    rhs = jnp.concatenate([jnp.concatenate(his, axis=1), jnp.concatenate(los, axis=1)], axis=0)
    prefix_all = jnp.dot(sm_ref[0:c, :], rhs, preferred_element_type=F32)
    direct_ok = jnp.min(jnp.concatenate(totals, axis=1)) >= -HGRN_DIRECT_LOG2_MAX

    def finish(h, out, state, inp, key_tail, last_decay):
        cols = slice(h * hw, (h + 1) * hw)
        st_ref[h] = state * last_decay + lax.dot_general(inp, key_tail, TN_DIMS, preferred_element_type=F32)
        ms = jnp.mean(out * out, axis=-1, keepdims=True)
        o_ref[:, cols] = (out * lax.rsqrt(ms + RMS_EPS) * og_ref[...]
                          * g_ref[:, cols].astype(F32)).astype(o_ref.dtype)

    @pl.when(direct_ok)
    def _():
        keep = col <= row
        for h in range(n_heads):
            cols = slice(h * hw, (h + 1) * hw)
            prefix = prefix_all[:, cols]
            grow = jnp.exp2(-prefix)
            q_dec = (q_ref[:, cols].astype(F32) * jnp.exp2(prefix)).astype(BF16)
            key_grow = keys[h] * grow
            inp = i_ref[:, cols]
            state = st_ref[h]
            attn = jnp.where(keep, lax.dot_general(q_dec, key_grow.astype(BF16), NT_DIMS,
                                                   preferred_element_type=F32), 0.0)
            out = (lax.dot_general(q_dec, state.astype(BF16), NT_DIMS, preferred_element_type=F32)
                   + jnp.dot(attn.astype(BF16), inp, preferred_element_type=F32))
            last_decay = jnp.exp2(prefix[c - 1:c, :])
            finish(h, out, state, inp, (key_grow * last_decay).astype(BF16), last_decay)

    @pl.when(jnp.logical_not(direct_ok))
    def _():
        lower = col < row
        split_bit = row ^ col
        level_masks = [lower & ((split_bit >> lev) == 1) for lev in range(HGRN_LEVELS)]
        sums_all = jnp.dot(sm_ref[c:, :], rhs, preferred_element_type=F32)
        for h in range(n_heads):
            cols = slice(h * hw, (h + 1) * hw)
            sums = sums_all[:, cols]
            prefix = prefix_all[:, cols]
            key = keys[h]
            q = q_ref[:, cols].astype(F32)
            inp = i_ref[:, cols]
            state = st_ref[h]
            out = lax.dot_general((q * jnp.exp2(prefix)).astype(BF16), state.astype(BF16), NT_DIMS,
                                  preferred_element_type=F32)
            attn = jnp.where(row == col,
                             lax.dot_general(q.astype(BF16), key.astype(BF16), NT_DIMS,
                                             preferred_element_type=F32), 0.0)
            for lev in range(HGRN_LEVELS):
                w = jnp.exp2(sums[(1 + lev) * c:(2 + lev) * c])
                pair = lax.dot_general((q * w).astype(BF16), (key * w).astype(BF16), NT_DIMS,
                                       preferred_element_type=F32)
                attn = jnp.where(level_masks[lev], pair, attn)
            out = out + jnp.dot(attn.astype(BF16), inp, preferred_element_type=F32)
            finish(h, out, state, inp, (key * jnp.exp2(sums[0:c])).astype(BF16),
                   jnp.exp2(prefix[c - 1:c, :]))


def _hgrn(q, f_pre, inp, gate, lb_logits, o_gain, layer, n_heads):
    b, s, _ = q.shape
    c = HGRN_CHUNK
    assert n_heads % HGRN_HEADS == 0 and s % c == 0
    w = HGRN_HEADS * HEAD_DIM
    groups = n_heads // HGRN_HEADS
    depth = lb_logits.shape[0]
    tok = lambda off: pl.BlockSpec((None, c, w), lambda bi, h, ci: (bi, ci, off + h))
    return pl.pallas_call(
        functools.partial(_hgrn_kernel, layer),
        grid=(b, groups, s // c),
        in_specs=[tok(0), tok(0), tok(0), tok(0),
                  pl.BlockSpec((depth, w), lambda bi, h, ci: (0, h)),
                  pl.BlockSpec((1, HEAD_DIM), lambda bi, h, ci: (0, 0)),
                  pl.BlockSpec(((2 + HGRN_LEVELS) * c, 2 * c), lambda bi, h, ci: (0, 0))],
        out_specs=tok(0),
        out_shape=jax.ShapeDtypeStruct((b, s, n_heads * HEAD_DIM), BF16),
        scratch_shapes=[pltpu.VMEM((HGRN_HEADS, HEAD_DIM, HEAD_DIM), F32)],
        compiler_params=_compiler_params(("parallel", "parallel", "arbitrary")),
        name="hgrn2",
    )(q, f_pre, inp, gate, lb_logits, o_gain.reshape(1, HEAD_DIM), _hgrn_sum_matrix())


def _even_layer(x, norm_g, w_in, q_gain, k_gain, w_out):
    b, s, d = x.shape
    d_inner = w_out.shape[0]
    d_half = d_inner // 2
    n_heads = d_half // HEAD_DIM
    m = b * s
    x2d = x.reshape(m, d)
    hdn = _rmsnorm(x2d, norm_g)
    c_sb, c_dil_qk, c_dil_v = 3 * d_half, 5 * d_half, 6 * d_half
    q_scale = math.log2(math.e) / math.sqrt(HEAD_DIM)
    sb_scale = jnp.concatenate([jnp.full((1, d_half), q_scale, F32), jnp.ones((1, 2 * d_half), F32)], axis=1)
    qkv_sb = _project(hdn, w_in, 0, c_sb, BF16, col_scale=sb_scale)
    head_gain = jnp.concatenate([jnp.tile(q_gain, n_heads), jnp.tile(k_gain, n_heads)]).reshape(1, 2 * d_half)
    qk_dil = _project(hdn, w_in, c_sb, c_dil_qk - c_sb, F32, head_gain=head_gain)
    v_dil = _project(hdn, w_in, c_dil_qk, c_dil_v - c_dil_qk, F32)
    gate = _project(hdn, w_in, c_dil_v, d_inner, BF16, silu=True).reshape(b, s, d_inner)
    mix_sb = _sb_attention(qkv_sb.reshape(b, s, c_sb), gate, n_heads)
    mix_dil = _dilated_attention(qk_dil.reshape(b, s, 2 * d_half), v_dil.reshape(b, s, d_half),
                                 gate, n_heads, n_heads, q_gain, k_gain)
    y = _out_project(x2d, [mix_sb.reshape(m, d_half), mix_dil.reshape(m, d_half)], w_out)
    return y.reshape(b, s, d)


def _odd_layer(x, norm_g, w_in, lb_logits, layer, o_gain, w_out):
    b, s, d = x.shape
    d_inner = w_out.shape[0]
    n_heads = d_inner // HEAD_DIM
    m = b * s
    x2d = x.reshape(m, d)
    hdn = _rmsnorm(x2d, norm_g)
    q = _project(hdn, w_in, 0, d_inner, BF16)
    f_pre = _project(hdn, w_in, d_inner, d_inner, F32)
    inp = _project(hdn, w_in, 2 * d_inner, d_inner, BF16)
    gate = _project(hdn, w_in, 3 * d_inner, d_inner, BF16, silu=True)
    mix = _hgrn(q.reshape(b, s, d_inner), f_pre.reshape(b, s, d_inner), inp.reshape(b, s, d_inner),
                gate.reshape(b, s, d_inner), lb_logits, o_gain, layer, n_heads)
    y = _out_project(x2d, [mix.reshape(m, d_inner)], w_out)
    return y.reshape(b, s, d)


def kernel(x, norm_even, w_in_even, q_norm_even, k_norm_even, w_out_even,
           norm_odd, w_in_odd, lb_logits, o_norm_odd, w_out_odd):
    depth = lb_logits.shape[0]
    for layer in range(depth):
        j = layer // 2
        if layer % 2 == 0:
            x = _even_layer(x, norm_even[j], w_in_even[j], q_norm_even[j], k_norm_even[j], w_out_even[j])
        else:
            x = _odd_layer(x, norm_odd[j], w_in_odd[j], lb_logits, layer, o_norm_odd[j], w_out_odd[j])
    return x
```
